```python
import math
import jax, jax.numpy as jnp
from jax import lax
import numpy as np

D_MODEL = 1024
BATCH = 8
SEQ = 2048
DEPTH = 4

GRID_W = 64
D_MIX = D_MODEL
DA_HEAD_DIM = 64
DA_WIDTH = D_MIX // 2
DA_HEADS = DA_WIDTH // (2 * DA_HEAD_DIM)
NA_HEAD_DIM = 64
NA_WIDTH = D_MIX - DA_WIDTH
NA_HEADS = NA_WIDTH // NA_HEAD_DIM
NA_ROWS_MAX = 8
NA_COLS = 16
NA_COL_BLOCK = 16
NA_KEY_COLS = NA_COL_BLOCK + NA_COLS
Q_BLOCK = 128
RMS_EPS = 1e-6
SPLITS = (DA_WIDTH, DA_WIDTH, DA_WIDTH, NA_WIDTH, NA_WIDTH, NA_WIDTH, D_MIX)
IN_WIDTH = sum(SPLITS)

kernel_name = "hybrid_diffattn_natten2d_encoder"


def rms_norm(x, g):
    xf = x.astype(jnp.float32)
    y = xf * lax.rsqrt(jnp.mean(xf * xf, axis=-1, keepdims=True) + RMS_EPS)
    return (y * g.astype(jnp.float32)).astype(x.dtype)


def alibi_slopes(n_heads):
    return jnp.asarray(2.0 ** (-8.0 * np.arange(1, n_heads + 1) / n_heads), dtype=jnp.float32)


def diff_attention(q, k, v, lam, lam_init, subln_g):
    B, S, H, _, dh = q.shape
    nb = S // Q_BLOCK
    qb = q.reshape(B, nb, Q_BLOCK, H, 2, dh).transpose(1, 0, 2, 3, 4, 5)
    slopes = alibi_slopes(H)
    kpos = jnp.arange(S)
    scale = dh ** -0.5

    def block(args):
        qi, i = args
        s = jnp.einsum('bqhcd,bkhcd->bhcqk', qi, k,
                       preferred_element_type=jnp.float32) * scale
        qpos = i * Q_BLOCK + jnp.arange(Q_BLOCK)
        dist = jnp.abs(qpos[:, None] - kpos[None, :]).astype(jnp.float32)
        s = s - slopes[None, :, None, None, None] * dist[None, None, None]
        p = jax.nn.softmax(s, axis=-1)
        a = p[:, :, 0] - lam * p[:, :, 1]
        return jnp.einsum('bhqk,bkhe->bqhe', a.astype(v.dtype), v)

    o = lax.map(block, (qb, jnp.arange(nb)))
    o = o.transpose(1, 0, 2, 3, 4).reshape(B, S, H, 2 * dh)
    o = rms_norm(o, subln_g) * (1.0 - lam_init)
    return o.reshape(B, S, H * 2 * dh)


def neighbourhood_attention(q, k, v, rpb):
    B, S, H, d = q.shape
    rows = S // GRID_W
    kr = min(NA_ROWS_MAX, rows)
    qg = q.reshape(B, rows, GRID_W, H, d)
    kg = k.reshape(B, rows, GRID_W, H, d)
    vg = v.reshape(B, rows, GRID_W, H, d)
    nbk = GRID_W // NA_COL_BLOCK
    qcol = np.arange(GRID_W).reshape(nbk, NA_COL_BLOCK)
    qcstart = np.clip(qcol - NA_COLS // 2, 0, GRID_W - NA_COLS)
    kcstart = np.clip(qcol[:, 0] - NA_COLS // 2, 0, GRID_W - NA_KEY_COLS)
    kcol = kcstart[:, None] + np.arange(NA_KEY_COLS)
    col_valid = ((kcol[:, None, :] >= qcstart[:, :, None]) &
                 (kcol[:, None, :] < qcstart[:, :, None] + NA_COLS))
    dc_idx = np.clip(kcol[:, None, :] - qcol[:, :, None],
                     -(NA_COLS - 1), NA_COLS - 1) + NA_COLS - 1
    col_bias = rpb[:, :, dc_idx]
    scale = d ** -0.5

    def row(r):
        rs = jnp.clip(r - kr // 2, 0, rows - kr)
        krows = lax.dynamic_slice_in_dim(kg, rs, kr, axis=1)
        vrows = lax.dynamic_slice_in_dim(vg, rs, kr, axis=1)
        kb = krows[:, :, kcol]
        vb = vrows[:, :, kcol]
        qr = lax.dynamic_index_in_dim(qg, r, axis=1, keepdims=False)
        qr = qr.reshape(B, nbk, NA_COL_BLOCK, H, d)
        s = jnp.einsum('bnqhd,brnkhd->bhnqrk', qr, kb,
                       preferred_element_type=jnp.float32) * scale
        dr_idx = rs + jnp.arange(kr) - r + NA_ROWS_MAX - 1
        bias = jnp.take(col_bias, dr_idx, axis=1).transpose(0, 2, 3, 1, 4)
        s = s + bias[None].astype(jnp.float32)
        s = jnp.where(col_valid[None, None, :, :, None, :], s, -jnp.inf)
        shp = s.shape
        p = jax.nn.softmax(s.reshape(B, H, nbk, NA_COL_BLOCK, kr * NA_KEY_COLS), axis=-1).reshape(shp)
        o = jnp.einsum('bhnqrk,brnkhd->bnqhd', p.astype(v.dtype), vb)
        return o.reshape(B, GRID_W, H * d)

    out = lax.map(row, jnp.arange(rows))
    return out.transpose(1, 0, 2, 3).reshape(B, S, H * d)


def hybrid_layer(x, g, w_in, w_out, lq1, lk1, lq2, lk2, subln_g, rpb, lam_init):
    B, S, _ = x.shape
    h = rms_norm(x, g)
    z = jnp.einsum('bsd,de->bse', h, w_in)
    cuts = [int(c) for c in np.cumsum(SPLITS)[:-1]]
    qd, kd, vd, qn, kn, vn, gate = jnp.split(z, cuts, axis=-1)
    qd = qd.reshape(B, S, DA_HEADS, 2, DA_HEAD_DIM)
    kd = kd.reshape(B, S, DA_HEADS, 2, DA_HEAD_DIM)
    vd = vd.reshape(B, S, DA_HEADS, 2 * DA_HEAD_DIM)
    f32 = jnp.float32
    lam = (jnp.exp(jnp.sum(lq1.astype(f32) * lk1.astype(f32)))
           - jnp.exp(jnp.sum(lq2.astype(f32) * lk2.astype(f32))) + lam_init)
    o_da = diff_attention(qd, kd, vd, lam, lam_init, subln_g)
    qn = qn.reshape(B, S, NA_HEADS, NA_HEAD_DIM)
    kn = kn.reshape(B, S, NA_HEADS, NA_HEAD_DIM)
    vn = vn.reshape(B, S, NA_HEADS, NA_HEAD_DIM)
    o_na = neighbourhood_attention(qn, kn, vn, rpb)
    o = jnp.concatenate([o_da, o_na], axis=-1) * jax.nn.silu(gate)
    return x + jnp.einsum('bse,ed->bsd', o, w_out)


def setup_inputs(seed: int = 0) -> dict:
    key = jax.random.key(seed)
    ks = jax.random.split(key, 12)
    f32 = jnp.float32
    x = jax.random.normal(ks[0], (BATCH, SEQ, D_MODEL), f32)
    norm_g = 1.0 + 0.02 * jax.random.normal(ks[1], (DEPTH, D_MODEL), f32)
    w_in = jax.random.normal(ks[2], (DEPTH, D_MODEL, IN_WIDTH), f32) * D_MODEL ** -0.5
    w_out = jax.random.normal(ks[3], (DEPTH, D_MIX, D_MODEL), f32) * D_MIX ** -0.5
    lam_q1 = 0.1 * jax.random.normal(ks[4], (DEPTH, DA_HEAD_DIM), f32)
    lam_k1 = 0.1 * jax.random.normal(ks[5], (DEPTH, DA_HEAD_DIM), f32)
    lam_q2 = 0.1 * jax.random.normal(ks[6], (DEPTH, DA_HEAD_DIM), f32)
    lam_k2 = 0.1 * jax.random.normal(ks[7], (DEPTH, DA_HEAD_DIM), f32)
    subln_g = 1.0 + 0.02 * jax.random.normal(ks[8], (DEPTH, 2 * DA_HEAD_DIM), f32)
    rpb = 0.02 * jax.random.normal(ks[9], (DEPTH, NA_HEADS, 2 * NA_ROWS_MAX - 1, 2 * NA_COLS - 1), f32)
    final_g = 1.0 + 0.02 * jax.random.normal(ks[10], (D_MODEL,), f32)
    return {"x": x, "norm_g": norm_g, "w_in": w_in, "w_out": w_out,
            "lam_q1": lam_q1, "lam_k1": lam_k1, "lam_q2": lam_q2, "lam_k2": lam_k2,
            "subln_g": subln_g, "rpb": rpb, "final_g": final_g}


def reference(x, norm_g, w_in, w_out, lam_q1, lam_k1, lam_q2, lam_k2, subln_g, rpb, final_g):
    for l in range(DEPTH):
        lam_init = 0.8 - 0.6 * math.exp(-0.3 * l)
        x = hybrid_layer(x, norm_g[l], w_in[l], w_out[l], lam_q1[l], lam_k1[l],
                         lam_q2[l], lam_k2[l], subln_g[l], rpb[l], lam_init)
    return rms_norm(x, final_g)
```

```python
import functools
import math

import numpy as np
import jax
import jax.numpy as jnp
from jax import lax
from jax.experimental import pallas as pl
from jax.experimental.pallas import tpu as pltpu

F32 = jnp.float32
BF16 = jnp.bfloat16

D_MODEL = 1024
DEPTH = 4
GRID_W = 64
DA_HEADS = 4
DA_HEAD_DIM = 64
NA_HEADS = 8
NA_HEAD_DIM = 64
NA_ROWS_MAX = 8
NA_COLS = 16
RMS_EPS = 1e-6
IN_WIDTH = 4096
LANES = 128

QD_BLK, KD_BLK, VD_BLK = 0, 4, 8
QN_BLK, KN_BLK, VN_BLK = 12, 16, 20
GATE_BLK = 3

TM_PROJ = 512
TQ_DA = 256
NA_QROWS = 4
NA_KROWS = 12
NA_TQ = NA_QROWS * GRID_W
NA_TK = NA_KROWS * GRID_W
N_RPB_DR = 2 * NA_ROWS_MAX - 1
N_RPB_DC = 2 * NA_COLS - 1
VMEM_LIMIT = 48 * 1024 * 1024


def _rms(x, g):
    return x * lax.rsqrt(jnp.mean(x * x, axis=-1, keepdims=True) + RMS_EPS) * g


def _inproj_kernel(x_ref, g_ref, w_ref, z_ref):
    h = _rms(x_ref[...], g_ref[...]).astype(BF16)
    for j in range(IN_WIDTH // 1024):
        cols = slice(j * 1024, (j + 1) * 1024)
        z_ref[:, cols] = jnp.dot(h, w_ref[:, cols], preferred_element_type=F32).astype(BF16)


def _inproj(x2, g, w_bf16):
    m = x2.shape[0]
    return pl.pallas_call(
        _inproj_kernel,
        grid=(m // TM_PROJ,),
        in_specs=[
            pl.BlockSpec((TM_PROJ, D_MODEL), lambda i: (i, 0)),
            pl.BlockSpec((1, D_MODEL), lambda i: (0, 0)),
            pl.BlockSpec((D_MODEL, IN_WIDTH), lambda i: (0, 0)),
        ],
        out_specs=pl.BlockSpec((TM_PROJ, IN_WIDTH), lambda i: (i, 0)),
        out_shape=jax.ShapeDtypeStruct((m, IN_WIDTH), BF16),
        compiler_params=pltpu.CompilerParams(
            dimension_semantics=("arbitrary",), vmem_limit_bytes=VMEM_LIMIT),
        name="inproj",
    )(x2, g, w_bf16)


def _da_kernel(sc_ref, lamv_ref, g_ref, q_ref, k_ref, v_ref, o_ref, *, seq):
    h = pl.program_id(1)
    slope = sc_ref[h]
    lam_init = sc_ref[4]
    out_scale = sc_ref[5]
    lv = lamv_ref[...]
    lam = (jnp.exp(jnp.sum(lv[0:1] * lv[1:2], axis=-1, keepdims=True))
           - jnp.exp(jnp.sum(lv[2:3] * lv[3:4], axis=-1, keepdims=True)) + lam_init)
    lane = lax.broadcasted_iota(jnp.int32, (1, LANES), 1)
    kpos = lax.broadcasted_iota(jnp.int32, (1, seq), 1)
    qoff = lax.broadcasted_iota(jnp.int32, (TQ_DA, 1), 0)
    g = g_ref[...]
    nt = (((1,), (1,)), ((), ()))

    def body(i, carry):
        q0 = pl.multiple_of(i * TQ_DA, TQ_DA)
        q = q_ref[0, pl.ds(q0, TQ_DA), :] * BF16(DA_HEAD_DIM ** -0.5)
        k = k_ref[0]
        bias = slope * jnp.abs(qoff + q0 - kpos).astype(F32)
        es, cs = [], []
        for c in range(2):
            qc = jnp.where((lane < DA_HEAD_DIM) == (c == 0), q, jnp.zeros_like(q))
            s = lax.dot_general(qc, k, nt, preferred_element_type=F32) - bias
            m = jnp.max(s, axis=-1, keepdims=True)
            e = jnp.exp(s - m)
            es.append(e)
            cs.append(1.0 / jnp.sum(e, axis=-1, keepdims=True))
        a = es[0] * cs[0] - es[1] * (lam * cs[1])
        o = jnp.dot(a.astype(BF16), v_ref[0], preferred_element_type=F32)
        o = _rms(o, g) * out_scale
        o_ref[0, pl.ds(q0, TQ_DA), :] = o.astype(BF16)
        return carry

    lax.fori_loop(0, seq // TQ_DA, body, 0)


def _diff_attention(z3, scalars, lamv, subln_g):
    b, s, _ = z3.shape
    smem = pl.BlockSpec(memory_space=pltpu.SMEM)
    return pl.pallas_call(
        functools.partial(_da_kernel, seq=s),
        grid=(b, DA_HEADS),
        in_specs=[
            smem,
            pl.BlockSpec((4, DA_HEAD_DIM), lambda i, h: (0, 0)),
            pl.BlockSpec((1, 2 * DA_HEAD_DIM), lambda i, h: (0, 0)),
            pl.BlockSpec((1, s, LANES), lambda i, h: (i, 0, QD_BLK + h)),
            pl.BlockSpec((1, s, LANES), lambda i, h: (i, 0, KD_BLK + h)),
            pl.BlockSpec((1, s, LANES), lambda i, h: (i, 0, VD_BLK + h)),
        ],
        out_specs=pl.BlockSpec((1, s, LANES), lambda i, h: (i, 0, h)),
        out_shape=jax.ShapeDtypeStruct((b, s, DA_HEADS * LANES), BF16),
        compiler_params=pltpu.CompilerParams(
            dimension_semantics=("arbitrary", "arbitrary"), vmem_limit_bytes=VMEM_LIMIT),
        name="diff_attn",
    )(scalars, lamv, subln_g, z3, z3, z3)


def _na_block_geometry(rows):
    kr = min(NA_ROWS_MAX, rows)
    nblk = rows // NA_QROWS
    geo = []
    for rb in range(nblk):
        ks = min(max(NA_QROWS * rb - kr // 2, 0), rows - NA_KROWS)
        var = 0 if rb == 0 else (2 if rb == nblk - 1 else 1)
        geo.append((ks, var))
    return geo


def _na_table_dr(rows):
    kr = min(NA_ROWS_MAX, rows)
    geo = _na_block_geometry(rows)
    rep = {0: 0, 1: 1, 2: len(geo) - 1}
    out = {}
    for var, rb in rep.items():
        ks = geo[rb][0]
        for ri in range(NA_QROWS):
            qr = NA_QROWS * rb + ri
            rs = min(max(qr - kr // 2, 0), rows - kr)
            for kj in range(NA_KROWS):
                krow = ks + kj
                out[(var, ri, kj)] = (krow - qr + NA_ROWS_MAX - 1) if rs <= krow < rs + kr else None
    return out


def _na_tables_kernel(rpb_ref, tab_ref, dict_ref, *, rows):
    row = pl.program_id(0) * NA_HEADS + pl.program_id(1)
    qc = lax.broadcasted_iota(jnp.int32, (GRID_W, GRID_W), 0)
    kc = lax.broadcasted_iota(jnp.int32, (GRID_W, GRID_W), 1)
    diff = kc - qc + (NA_COLS - 1)
    qcs = jnp.clip(qc - NA_COLS // 2, 0, GRID_W - NA_COLS)
    col_valid = (kc >= qcs) & (kc < qcs + NA_COLS)
    neg = jnp.full((GRID_W, GRID_W), -jnp.inf, F32)
    for dr in range(N_RPB_DR):
        t = jnp.zeros((GRID_W, GRID_W), F32)
        for dc in range(N_RPB_DC):
            t = jnp.where(diff == dc, rpb_ref[row, dr * N_RPB_DC + dc], t)
        dict_ref[dr] = jnp.where(col_valid, t, neg)
    for (var, ri, kj), dr in _na_table_dr(rows).items():
        tile = neg if dr is None else dict_ref[dr]
        tab_ref[0, 0, var, ri * GRID_W:(ri + 1) * GRID_W, kj * GRID_W:(kj + 1) * GRID_W] = tile


def _na_tables(rpb, rows):
    depth = rpb.shape[0]
    rpb2 = rpb.reshape(depth * NA_HEADS, N_RPB_DR * N_RPB_DC)
    return pl.pallas_call(
        functools.partial(_na_tables_kernel, rows=rows),
        grid=(depth, NA_HEADS),
        in_specs=[pl.BlockSpec(memory_space=pltpu.SMEM)],
        out_specs=pl.BlockSpec((1, 1, 3, NA_TQ, NA_TK), lambda l, h: (l, h, 0, 0, 0)),
        out_shape=jax.ShapeDtypeStruct((depth, NA_HEADS, 3, NA_TQ, NA_TK), F32),
        scratch_shapes=[pltpu.VMEM((N_RPB_DR, GRID_W, GRID_W), F32)],
        compiler_params=pltpu.CompilerParams(
            dimension_semantics=("arbitrary", "arbitrary"), vmem_limit_bytes=VMEM_LIMIT),
        name="na_tables",
    )(rpb2)


def _na_kernel(tab_ref, q_ref, k_ref, v_ref, o_ref, *, rows):
    lane = lax.broadcasted_iota(jnp.int32, (1, LANES), 1)
    nt = (((1,), (1,)), ((), ()))
    for rb, (ks, var) in enumerate(_na_block_geometry(rows)):
        qs = slice(rb * NA_TQ, (rb + 1) * NA_TQ)
        kslc = slice(ks * GRID_W, ks * GRID_W + NA_TK)
        q = q_ref[0, qs, :] * BF16(NA_HEAD_DIM ** -0.5)
        kb = k_ref[0, kslc, :]
        vb = v_ref[0, kslc, :]
        outs = []
        for j in range(2):
            qj = jnp.where((lane < NA_HEAD_DIM) == (j == 0), q, jnp.zeros_like(q))
            s = lax.dot_general(qj, kb, nt, preferred_element_type=F32) + tab_ref[0, j, var]
            m = jnp.max(s, axis=-1, keepdims=True)
            e = jnp.exp(s - m)
            l = jnp.sum(e, axis=-1, keepdims=True)
            outs.append(jnp.dot(e.astype(BF16), vb, preferred_element_type=F32) * (1.0 / l))
        o = jnp.where(lane < NA_HEAD_DIM, outs[0], outs[1])
        o_ref[0, qs, :] = o.astype(BF16)


def _neigh_attention(z3, tables, layer):
    b, s, _ = z3.shape
    rows = s // GRID_W
    return pl.pallas_call(
        functools.partial(_na_kernel, rows=rows),
        grid=(NA_HEADS // 2, b),
        in_specs=[
            pl.BlockSpec((1, 2, 3, NA_TQ, NA_TK), lambda hp, i: (layer, hp, 0, 0, 0)),
            pl.BlockSpec((1, s, LANES), lambda hp, i: (i, 0, QN_BLK + hp)),
            pl.BlockSpec((1, s, LANES), lambda hp, i: (i, 0, KN_BLK + hp)),
            pl.BlockSpec((1, s, LANES), lambda hp, i: (i, 0, VN_BLK + hp)),
        ],
        out_specs=pl.BlockSpec((1, s, LANES), lambda hp, i: (i, 0, hp)),
        out_shape=jax.ShapeDtypeStruct((b, s, NA_HEADS * NA_HEAD_DIM), BF16),
        compiler_params=pltpu.CompilerParams(
            dimension_semantics=("arbitrary", "arbitrary"), vmem_limit_bytes=VMEM_LIMIT),
        name="neigh_attn",
    )(tables, z3, z3, z3)


def _outproj_kernel(x_ref, oda_ref, ona_ref, gate_ref, w_ref, fg_ref, xo_ref, *, final):
    gate = gate_ref[...].astype(F32)
    o = jnp.concatenate([oda_ref[...], ona_ref[...]], axis=-1).astype(F32) * (gate * jax.nn.sigmoid(gate))
    xo = x_ref[...] + jnp.dot(o.astype(BF16), w_ref[...], preferred_element_type=F32)
    if final:
        xo = _rms(xo, fg_ref[...])
    xo_ref[...] = xo


def _outproj(x2, oda2, ona2, z2, w_bf16, final_g, final):
    m = x2.shape[0]
    half = D_MODEL // 2
    return pl.pallas_call(
        functools.partial(_outproj_kernel, final=final),
        grid=(m // TM_PROJ,),
        in_specs=[
            pl.BlockSpec((TM_PROJ, D_MODEL), lambda i: (i, 0)),
            pl.BlockSpec((TM_PROJ, half), lambda i: (i, 0)),
            pl.BlockSpec((TM_PROJ, half), lambda i: (i, 0)),
            pl.BlockSpec((TM_PROJ, D_MODEL), lambda i: (i, GATE_BLK)),
            pl.BlockSpec((D_MODEL, D_MODEL), lambda i: (0, 0)),
            pl.BlockSpec((1, D_MODEL), lambda i: (0, 0)),
        ],
        out_specs=pl.BlockSpec((TM_PROJ, D_MODEL), lambda i: (i, 0)),
        out_shape=jax.ShapeDtypeStruct((m, D_MODEL), F32),
        compiler_params=pltpu.CompilerParams(
            dimension_semantics=("arbitrary",), vmem_limit_bytes=VMEM_LIMIT),
        name="outproj_final" if final else "outproj",
    )(x2, oda2, ona2, z2, w_bf16, final_g)


def kernel(x, norm_g, w_in, w_out, lam_q1, lam_k1, lam_q2, lam_k2, subln_g, rpb, final_g):
    b, s, d = x.shape
    depth = norm_g.shape[0]
    assert d == D_MODEL and s % TQ_DA == 0 and s % (GRID_W * NA_QROWS) == 0
    rows = s // GRID_W
    assert rows >= NA_KROWS
    m = b * s
    w_in_bf = w_in.astype(BF16)
    w_out_bf = w_out.astype(BF16)
    slopes = np.asarray(2.0 ** (-8.0 * np.arange(1, DA_HEADS + 1) / DA_HEADS), dtype=np.float32)
    tables = _na_tables(rpb, rows)
    fg = final_g.reshape(1, d)
    x2 = x.reshape(m, d)
    for l in range(depth):
        lam_init = 0.8 - 0.6 * math.exp(-0.3 * l)
        scalars = jnp.asarray(np.concatenate(
            [slopes, np.asarray([lam_init, 1.0 - lam_init, 0.0, 0.0], np.float32)]))
        lamv = jnp.stack([lam_q1[l], lam_k1[l], lam_q2[l], lam_k2[l]])
        z2 = _inproj(x2, norm_g[l].reshape(1, d), w_in_bf[l])
        z3 = z2.reshape(b, s, IN_WIDTH)
        oda = _diff_attention(z3, scalars, lamv, subln_g[l].reshape(1, -1))
        ona = _neigh_attention(z3, tables, l)
        x2 = _outproj(x2, oda.reshape(m, -1), ona.reshape(m, -1), z2, w_out_bf[l], fg,
                      final=(l == depth - 1))
    return x2.reshape(b, s, d)
```

```python
import functools
import math

import numpy as np
import jax
import jax.numpy as jnp
from jax import lax
from jax.experimental import pallas as pl
from jax.experimental.pallas import tpu as pltpu

F32 = jnp.float32
BF16 = jnp.bfloat16

D_MODEL = 1024
DEPTH = 4
GRID_W = 64
DA_HEADS = 4
DA_HEAD_DIM = 64
NA_HEADS = 8
NA_HEAD_DIM = 64
NA_ROWS_MAX = 8
NA_COLS = 16
RMS_EPS = 1e-6
IN_WIDTH = 4096
LANES = 128

QD_BLK, KD_BLK, VD_BLK = 0, 4, 8
QN_BLK, KN_BLK, VN_BLK = 12, 16, 20
GATE_BLK = 3

TM_PROJ = 512
TQ_DA = 256
TK_DA = TQ_DA
DA_STRIP = 16
NA_QROWS = 4
NA_KROWS = 12
NA_TQ = NA_QROWS * GRID_W
NA_TK = NA_KROWS * GRID_W
N_RPB_DR = 2 * NA_ROWS_MAX - 1
N_RPB_DC = 2 * NA_COLS - 1
VMEM_LIMIT = 48 * 1024 * 1024
LOG2E = math.log2(math.e)
Q_SCALE = DA_HEAD_DIM ** -0.5 * LOG2E
assert NA_HEAD_DIM == DA_HEAD_DIM
Z_CHUNK = 512
Q_CHUNKS = (QD_BLK * LANES // Z_CHUNK, QN_BLK * LANES // Z_CHUNK)


def _rms(x, g):
    return x * lax.rsqrt(jnp.mean(x * x, axis=-1, keepdims=True) + RMS_EPS) * g


def _inproj_kernel(x_ref, g_ref, w_ref, z_ref):
    h = _rms(x_ref[...], g_ref[...]).astype(BF16)
    for j in range(IN_WIDTH // Z_CHUNK):
        cols = slice(j * Z_CHUNK, (j + 1) * Z_CHUNK)
        zc = jnp.dot(h, w_ref[:, cols], preferred_element_type=F32)
        if j in Q_CHUNKS:
            zc = zc * Q_SCALE
        z_ref[:, cols] = zc.astype(BF16)


def _inproj(x2, g, w_bf16):
    m = x2.shape[0]
    return pl.pallas_call(
        _inproj_kernel,
        grid=(m // TM_PROJ,),
        in_specs=[
            pl.BlockSpec((TM_PROJ, D_MODEL), lambda i: (i, 0)),
            pl.BlockSpec((1, D_MODEL), lambda i: (0, 0)),
            pl.BlockSpec((D_MODEL, IN_WIDTH), lambda i: (0, 0)),
        ],
        out_specs=pl.BlockSpec((TM_PROJ, IN_WIDTH), lambda i: (i, 0)),
        out_shape=jax.ShapeDtypeStruct((m, IN_WIDTH), BF16),
        compiler_params=pltpu.CompilerParams(
            dimension_semantics=("arbitrary",), vmem_limit_bytes=VMEM_LIMIT),
        name="inproj",
    )(x2, g, w_bf16)


def _da_kernel(sc_ref, lamv_ref, g_ref, q_ref, k_ref, v_ref, o_ref,
               bias_ref, vt_ref, sa_ref, sb_ref, *, seq):
    nq = seq // TQ_DA
    nk = seq // TK_DA
    h = pl.program_id(0)
    lam_init = sc_ref[4]
    out_scale = sc_ref[5]

    @pl.when(pl.program_id(1) == 0)
    def _():
        slope = sc_ref[h] * LOG2E
        kr = lax.broadcasted_iota(jnp.int32, (TK_DA, TQ_DA), 0)
        qc = lax.broadcasted_iota(jnp.int32, (TK_DA, TQ_DA), 1)
        for t in range(nq + nk - 1):
            bias_ref[t] = slope * jnp.abs(kr - qc + (t - (nq - 1)) * TQ_DA).astype(F32)

    vt_ref[...] = v_ref[0].T
    lv = lamv_ref[...]
    lam = (jnp.exp(jnp.sum(lv[0:1] * lv[1:2], axis=-1, keepdims=True))
           - jnp.exp(jnp.sum(lv[2:3] * lv[3:4], axis=-1, keepdims=True)) + lam_init)
    lane = lax.broadcasted_iota(jnp.int32, (1, LANES), 1)
    g = g_ref[...]
    nt = (((1,), (1,)), ((), ()))

    def rows_of(blk):
        if isinstance(blk, int):
            return slice(blk * TQ_DA, (blk + 1) * TQ_DA)
        return pl.ds(pl.multiple_of(blk * TQ_DA, TQ_DA), TQ_DA)

    def scores(blk, s_ref):
        q = q_ref[0, rows_of(blk), :]
        zero = jnp.zeros_like(q)
        qst = jnp.concatenate([jnp.where(lane < DA_HEAD_DIM, q, zero),
                               jnp.where(lane < DA_HEAD_DIM, zero, q)], axis=0)
        bias = jnp.concatenate([bias_ref[j - blk + nq - 1] for j in range(nk)], axis=0)
        s = lax.dot_general(k_ref[0], qst, nt, preferred_element_type=F32)
        s = s - jnp.concatenate([bias, bias], axis=1)
        s_ref[...] = s
        return jnp.max(s, axis=0, keepdims=True)

    def attend(blk, s_ref, m):
        e = jnp.exp2(s_ref[...] - m)
        l = jnp.sum(e, axis=0, keepdims=True)
        pv = jnp.dot(vt_ref[...], e.astype(BF16), preferred_element_type=F32)
        ot = pv[:, :TQ_DA] * (1.0 / l[:, :TQ_DA]) - pv[:, TQ_DA:] * (lam / l[:, TQ_DA:])
        ot = ot * lax.rsqrt(jnp.mean(ot * ot, axis=0, keepdims=True) + RMS_EPS) * g * out_scale
        o_ref[0, rows_of(blk), :] = ot.T.astype(BF16)

    def pair(p, m_even):
        m_odd = scores(2 * p + 1, sb_ref)
        attend(2 * p, sa_ref, m_even)
        m_next = scores(2 * p + 2, sa_ref)
        attend(2 * p + 1, sb_ref, m_odd)
        return m_next

    m_even = lax.fori_loop(0, nq // 2 - 1, pair, scores(0, sa_ref))
    m_odd = scores(nq - 1, sb_ref)
    attend(nq - 2, sa_ref, m_even)
    attend(nq - 1, sb_ref, m_odd)


def _diff_attention(z3, scalars, lamv, subln_g):
    b, s, _ = z3.shape
    smem = pl.BlockSpec(memory_space=pltpu.SMEM)
    nq, nk = s // TQ_DA, s // TK_DA
    return pl.pallas_call(
        functools.partial(_da_kernel, seq=s),
        grid=(DA_HEADS, b),
        in_specs=[
            smem,
            pl.BlockSpec((4, DA_HEAD_DIM), lambda h, i: (0, 0)),
            pl.BlockSpec((2 * DA_HEAD_DIM, 1), lambda h, i: (0, 0)),
            pl.BlockSpec((1, s, LANES), lambda h, i: (i, 0, QD_BLK + h)),
            pl.BlockSpec((1, s, LANES), lambda h, i: (i, 0, KD_BLK + h)),
            pl.BlockSpec((1, s, LANES), lambda h, i: (i, 0, VD_BLK + h)),
        ],
        out_specs=pl.BlockSpec((1, s, LANES), lambda h, i: (i, 0, h)),
        out_shape=jax.ShapeDtypeStruct((b, s, DA_HEADS * LANES), BF16),
        scratch_shapes=[
            pltpu.VMEM((nq + nk - 1, TK_DA, TQ_DA), F32),
            pltpu.VMEM((LANES, s), BF16),
            pltpu.VMEM((s, 2 * TQ_DA), F32),
            pltpu.VMEM((s, 2 * TQ_DA), F32),
        ],
        compiler_params=pltpu.CompilerParams(
            dimension_semantics=("arbitrary", "arbitrary"), vmem_limit_bytes=VMEM_LIMIT),
        name="diff_attn",
    )(scalars, lamv, subln_g, z3, z3, z3)


def _na_block_geometry(rows):
    kr = min(NA_ROWS_MAX, rows)
    nblk = rows // NA_QROWS
    geo = []
    for rb in range(nblk):
        ks = min(max(NA_QROWS * rb - kr // 2, 0), rows - NA_KROWS)
        var = 0 if rb == 0 else (2 if rb == nblk - 1 else 1)
        geo.append((ks, var))
    return geo


def _na_table_dr(rows):
    kr = min(NA_ROWS_MAX, rows)
    geo = _na_block_geometry(rows)
    rep = {0: 0, 1: 1, 2: len(geo) - 1}
    out = {}
    for var, rb in rep.items():
        ks = geo[rb][0]
        for ri in range(NA_QROWS):
            qr = NA_QROWS * rb + ri
            rs = min(max(qr - kr // 2, 0), rows - kr)
            for kj in range(NA_KROWS):
                krow = ks + kj
                out[(var, ri, kj)] = (krow - qr + NA_ROWS_MAX - 1) if rs <= krow < rs + kr else None
    return out


def _na_tables_kernel(rpb_ref, tab_ref, dict_ref, *, rows):
    row = pl.program_id(0) * NA_HEADS + pl.program_id(1)
    qc = lax.broadcasted_iota(jnp.int32, (GRID_W, GRID_W), 0)
    kc = lax.broadcasted_iota(jnp.int32, (GRID_W, GRID_W), 1)
    diff = kc - qc + (NA_COLS - 1)
    qcs = jnp.clip(qc - NA_COLS // 2, 0, GRID_W - NA_COLS)
    col_valid = (kc >= qcs) & (kc < qcs + NA_COLS)
    neg = jnp.full((GRID_W, GRID_W), -jnp.inf, F32)
    for dr in range(N_RPB_DR):
        t = jnp.zeros((GRID_W, GRID_W), F32)
        for dc in range(N_RPB_DC):
            t = jnp.where(diff == dc, rpb_ref[row, dr * N_RPB_DC + dc], t)
        dict_ref[dr] = jnp.where(col_valid, t * LOG2E, neg)
    for (var, ri, kj), dr in _na_table_dr(rows).items():
        tile = neg if dr is None else dict_ref[dr]
        tab_ref[0, 0, var, ri * GRID_W:(ri + 1) * GRID_W, kj * GRID_W:(kj + 1) * GRID_W] = tile


def _na_tables(rpb, rows):
    depth = rpb.shape[0]
    rpb2 = rpb.reshape(depth * NA_HEADS, N_RPB_DR * N_RPB_DC)
    return pl.pallas_call(
        functools.partial(_na_tables_kernel, rows=rows),
        grid=(depth, NA_HEADS),
        in_specs=[pl.BlockSpec(memory_space=pltpu.SMEM)],
        out_specs=pl.BlockSpec((1, 1, 3, NA_TQ, NA_TK), lambda l, h: (l, h, 0, 0, 0)),
        out_shape=jax.ShapeDtypeStruct((depth, NA_HEADS, 3, NA_TQ, NA_TK), F32),
        scratch_shapes=[pltpu.VMEM((N_RPB_DR, GRID_W, GRID_W), F32)],
        compiler_params=pltpu.CompilerParams(
            dimension_semantics=("arbitrary", "arbitrary"), vmem_limit_bytes=VMEM_LIMIT),
        name="na_tables",
    )(rpb2)


def _na_kernel(tab_ref, q_ref, k_ref, v_ref, o_ref, *, rows):
    lane = lax.broadcasted_iota(jnp.int32, (1, LANES), 1)
    nt = (((1,), (1,)), ((), ()))
    for rb, (ks, var) in enumerate(_na_block_geometry(rows)):
        qs = slice(rb * NA_TQ, (rb + 1) * NA_TQ)
        kslc = slice(ks * GRID_W, ks * GRID_W + NA_TK)
        q = q_ref[0, qs, :]
        kb = k_ref[0, kslc, :]
        vb = v_ref[0, kslc, :]
        outs = []
        for j in range(2):
            qj = jnp.where((lane < NA_HEAD_DIM) == (j == 0), q, jnp.zeros_like(q))
            s = lax.dot_general(qj, kb, nt, preferred_element_type=F32) + tab_ref[0, j, var]
            m = jnp.max(s, axis=-1, keepdims=True)
            e = jnp.exp2(s - m)
            l = jnp.sum(e, axis=-1, keepdims=True)
            outs.append(jnp.dot(e.astype(BF16), vb, preferred_element_type=F32) * (1.0 / l))
        o = jnp.where(lane < NA_HEAD_DIM, outs[0], outs[1])
        o_ref[0, qs, :] = o.astype(BF16)


def _neigh_attention(z3, tables, layer):
    b, s, _ = z3.shape
    rows = s // GRID_W
    return pl.pallas_call(
        functools.partial(_na_kernel, rows=rows),
        grid=(NA_HEADS // 2, b),
        in_specs=[
            pl.BlockSpec((1, 2, 3, NA_TQ, NA_TK), lambda hp, i: (layer, hp, 0, 0, 0)),
            pl.BlockSpec((1, s, LANES), lambda hp, i: (i, 0, QN_BLK + hp)),
            pl.BlockSpec((1, s, LANES), lambda hp, i: (i, 0, KN_BLK + hp)),
            pl.BlockSpec((1, s, LANES), lambda hp, i: (i, 0, VN_BLK + hp)),
        ],
        out_specs=pl.BlockSpec((1, s, LANES), lambda hp, i: (i, 0, hp)),
        out_shape=jax.ShapeDtypeStruct((b, s, NA_HEADS * NA_HEAD_DIM), BF16),
        compiler_params=pltpu.CompilerParams(
            dimension_semantics=("arbitrary", "arbitrary"), vmem_limit_bytes=VMEM_LIMIT),
        name="neigh_attn",
    )(tables, z3, z3, z3)


def _outproj_kernel(x_ref, oda_ref, ona_ref, gate_ref, w_ref, fg_ref, xo_ref, *, final):
    gate = gate_ref[...].astype(F32)
    o = jnp.concatenate([oda_ref[...], ona_ref[...]], axis=-1).astype(F32) * (gate * jax.nn.sigmoid(gate))
    xo = x_ref[...] + jnp.dot(o.astype(BF16), w_ref[...], preferred_element_type=F32)
    if final:
        xo = _rms(xo, fg_ref[...])
    xo_ref[...] = xo


def _outproj(x2, oda2, ona2, z2, w_bf16, final_g, final):
    m = x2.shape[0]
    half = D_MODEL // 2
    return pl.pallas_call(
        functools.partial(_outproj_kernel, final=final),
        grid=(m // TM_PROJ,),
        in_specs=[
            pl.BlockSpec((TM_PROJ, D_MODEL), lambda i: (i, 0)),
            pl.BlockSpec((TM_PROJ, half), lambda i: (i, 0)),
            pl.BlockSpec((TM_PROJ, half), lambda i: (i, 0)),
            pl.BlockSpec((TM_PROJ, D_MODEL), lambda i: (i, GATE_BLK)),
            pl.BlockSpec((D_MODEL, D_MODEL), lambda i: (0, 0)),
            pl.BlockSpec((1, D_MODEL), lambda i: (0, 0)),
        ],
        out_specs=pl.BlockSpec((TM_PROJ, D_MODEL), lambda i: (i, 0)),
        out_shape=jax.ShapeDtypeStruct((m, D_MODEL), F32),
        compiler_params=pltpu.CompilerParams(
            dimension_semantics=("arbitrary",), vmem_limit_bytes=VMEM_LIMIT),
        name="outproj_final" if final else "outproj",
    )(x2, oda2, ona2, z2, w_bf16, final_g)


def kernel(x, norm_g, w_in, w_out, lam_q1, lam_k1, lam_q2, lam_k2, subln_g, rpb, final_g):
    b, s, d = x.shape
    depth = norm_g.shape[0]
    assert d == D_MODEL and s % TQ_DA == 0 and s % (GRID_W * NA_QROWS) == 0
    rows = s // GRID_W
    assert rows >= NA_KROWS
    m = b * s
    w_in_bf = w_in.astype(BF16)
    w_out_bf = w_out.astype(BF16)
    slopes = np.asarray(2.0 ** (-8.0 * np.arange(1, DA_HEADS + 1) / DA_HEADS), dtype=np.float32)
    tables = _na_tables(rpb, rows)
    fg = final_g.reshape(1, d)
    x2 = x.reshape(m, d)
    for l in range(depth):
        lam_init = 0.8 - 0.6 * math.exp(-0.3 * l)
        scalars = jnp.asarray(np.concatenate(
            [slopes, np.asarray([lam_init, 1.0 - lam_init, 0.0, 0.0], np.float32)]))
        lamv = jnp.stack([lam_q1[l], lam_k1[l], lam_q2[l], lam_k2[l]])
        z2 = _inproj(x2, norm_g[l].reshape(1, d), w_in_bf[l])
        z3 = z2.reshape(b, s, IN_WIDTH)
        oda = _diff_attention(z3, scalars, lamv, subln_g[l].reshape(-1, 1))
        ona = _neigh_attention(z3, tables, l)
        x2 = _outproj(x2, oda.reshape(m, -1), ona.reshape(m, -1), z2, w_out_bf[l], fg,
                      final=(l == depth - 1))
    return x2.reshape(b, s, d)
```

```python
import functools
import math

import numpy as np
import jax
import jax.numpy as jnp
from jax import lax
from jax.experimental import pallas as pl
from jax.experimental.pallas import tpu as pltpu

F32 = jnp.float32
BF16 = jnp.bfloat16

D_MODEL = 1024
DEPTH = 4
GRID_W = 64
DA_HEADS = 4
DA_HEAD_DIM = 64
NA_HEADS = 8
NA_HEAD_DIM = 64
NA_ROWS_MAX = 8
NA_COLS = 16
RMS_EPS = 1e-6
IN_WIDTH = 4096
LANES = 128

QD_BLK, KD_BLK, VD_BLK = 0, 4, 8
QN_BLK, KN_BLK, VN_BLK = 12, 16, 20
GATE_BLK = 3

TM_PROJ = 512
TQ_DA = 256
TK_DA = TQ_DA
SUM_ROWS = 16
NA_QROWS = 4
NA_KROWS = 12
NA_TQ = NA_QROWS * GRID_W
NA_TK = NA_KROWS * GRID_W
N_RPB_DR = 2 * NA_ROWS_MAX - 1
N_RPB_DC = 2 * NA_COLS - 1
VMEM_LIMIT = 48 * 1024 * 1024
LOG2E = math.log2(math.e)
Q_SCALE = DA_HEAD_DIM ** -0.5 * LOG2E
assert NA_HEAD_DIM == DA_HEAD_DIM
Z_CHUNK = 512
Q_CHUNKS = (QD_BLK * LANES // Z_CHUNK, QN_BLK * LANES // Z_CHUNK)


def _rms(x, g):
    return x * lax.rsqrt(jnp.mean(x * x, axis=-1, keepdims=True) + RMS_EPS) * g


def _inproj_kernel(x_ref, g_ref, w_ref, z_ref):
    h = _rms(x_ref[...], g_ref[...]).astype(BF16)
    for j in range(IN_WIDTH // Z_CHUNK):
        cols = slice(j * Z_CHUNK, (j + 1) * Z_CHUNK)
        zc = jnp.dot(h, w_ref[:, cols], preferred_element_type=F32)
        if j in Q_CHUNKS:
            zc = zc * Q_SCALE
        z_ref[:, cols] = zc.astype(BF16)


def _inproj(x2, g, w_bf16):
    m = x2.shape[0]
    return pl.pallas_call(
        _inproj_kernel,
        grid=(m // TM_PROJ,),
        in_specs=[
            pl.BlockSpec((TM_PROJ, D_MODEL), lambda i: (i, 0)),
            pl.BlockSpec((1, D_MODEL), lambda i: (0, 0)),
            pl.BlockSpec((D_MODEL, IN_WIDTH), lambda i: (0, 0)),
        ],
        out_specs=pl.BlockSpec((TM_PROJ, IN_WIDTH), lambda i: (i, 0)),
        out_shape=jax.ShapeDtypeStruct((m, IN_WIDTH), BF16),
        compiler_params=pltpu.CompilerParams(
            dimension_semantics=("arbitrary",), vmem_limit_bytes=VMEM_LIMIT),
        name="inproj",
    )(x2, g, w_bf16)


def _fill_vt(vt_ref, v_ref):
    vt_ref[0:LANES, :] = v_ref[0].T
    vt_ref[LANES:, :] = jnp.ones((SUM_ROWS, vt_ref.shape[1]), BF16)


def _da_kernel(sc_ref, lamv_ref, g_ref, q_ref, k_ref, v_ref, o_ref,
               bias_ref, vt_ref, sa_ref, sb_ref, pa_ref, pb_ref, *, seq):
    nq = seq // TQ_DA
    nk = seq // TK_DA
    h = pl.program_id(0)
    lam_init = sc_ref[4]
    out_scale = sc_ref[5]

    @pl.when(pl.program_id(1) == 0)
    def _():
        slope = sc_ref[h] * LOG2E
        kr = lax.broadcasted_iota(jnp.int32, (TK_DA, TQ_DA), 0)
        qc = lax.broadcasted_iota(jnp.int32, (TK_DA, TQ_DA), 1)
        for t in range(nq + nk - 1):
            bias_ref[t] = slope * jnp.abs(kr - qc + (t - (nq - 1)) * TQ_DA).astype(F32)

    _fill_vt(vt_ref, v_ref)
    lv = lamv_ref[...]
    lam = (jnp.exp(jnp.sum(lv[0:1] * lv[1:2], axis=-1, keepdims=True))
           - jnp.exp(jnp.sum(lv[2:3] * lv[3:4], axis=-1, keepdims=True)) + lam_init)
    lane = lax.broadcasted_iota(jnp.int32, (1, LANES), 1)
    g = g_ref[...]
    nt = (((1,), (1,)), ((), ()))

    def rows_of(blk):
        if isinstance(blk, int):
            return slice(blk * TQ_DA, (blk + 1) * TQ_DA)
        return pl.ds(pl.multiple_of(blk * TQ_DA, TQ_DA), TQ_DA)

    def scores(blk, s_ref):
        q = q_ref[0, rows_of(blk), :]
        zero = jnp.zeros_like(q)
        qst = jnp.concatenate([jnp.where(lane < DA_HEAD_DIM, q, zero),
                               jnp.where(lane < DA_HEAD_DIM, zero, q)], axis=0)
        bias = jnp.concatenate([bias_ref[j - blk + nq - 1] for j in range(nk)], axis=0)
        s = lax.dot_general(k_ref[0], qst, nt, preferred_element_type=F32)
        s = s - jnp.concatenate([bias, bias], axis=1)
        s_ref[...] = s
        return jnp.max(s, axis=0, keepdims=True)

    def attend(s_ref, m, pv_ref):
        e = jnp.exp2(s_ref[...] - m)
        pv_ref[...] = jnp.dot(vt_ref[...], e.astype(BF16), preferred_element_type=F32)

    def finalize(blk, pv_ref):
        pv, l = pv_ref[0:LANES, :], pv_ref[LANES:LANES + 1, :]
        ot = pv[:, :TQ_DA] * (1.0 / l[:, :TQ_DA]) - pv[:, TQ_DA:] * (lam / l[:, TQ_DA:])
        ot = ot * lax.rsqrt(jnp.mean(ot * ot, axis=0, keepdims=True) + RMS_EPS) * g * out_scale
        o_ref[0, rows_of(blk), :] = ot.T.astype(BF16)

    def pair(p, m_even, first=False, last=False):
        if not first:
            finalize(2 * p - 1, pb_ref)
        m_odd = scores(2 * p + 1, sb_ref)
        attend(sa_ref, m_even, pa_ref)
        m_next = None if last else scores(2 * p + 2, sa_ref)
        finalize(2 * p, pa_ref)
        attend(sb_ref, m_odd, pb_ref)
        return m_next

    m_even = pair(0, scores(0, sa_ref), first=True)
    m_even = lax.fori_loop(1, nq // 2 - 1, pair, m_even)
    pair(nq // 2 - 1, m_even, last=True)
    finalize(nq - 1, pb_ref)


def _diff_attention(z3, scalars, lamv, subln_g):
    b, s, _ = z3.shape
    smem = pl.BlockSpec(memory_space=pltpu.SMEM)
    nq, nk = s // TQ_DA, s // TK_DA
    return pl.pallas_call(
        functools.partial(_da_kernel, seq=s),
        grid=(DA_HEADS, b),
        in_specs=[
            smem,
            pl.BlockSpec((4, DA_HEAD_DIM), lambda h, i: (0, 0)),
            pl.BlockSpec((2 * DA_HEAD_DIM, 1), lambda h, i: (0, 0)),
            pl.BlockSpec((1, s, LANES), lambda h, i: (i, 0, QD_BLK + h)),
            pl.BlockSpec((1, s, LANES), lambda h, i: (i, 0, KD_BLK + h)),
            pl.BlockSpec((1, s, LANES), lambda h, i: (i, 0, VD_BLK + h)),
        ],
        out_specs=pl.BlockSpec((1, s, LANES), lambda h, i: (i, 0, h)),
        out_shape=jax.ShapeDtypeStruct((b, s, DA_HEADS * LANES), BF16),
        scratch_shapes=[
            pltpu.VMEM((nq + nk - 1, TK_DA, TQ_DA), F32),
            pltpu.VMEM((LANES + SUM_ROWS, s), BF16),
            pltpu.VMEM((s, 2 * TQ_DA), F32),
            pltpu.VMEM((s, 2 * TQ_DA), F32),
            pltpu.VMEM((LANES + SUM_ROWS, 2 * TQ_DA), F32),
            pltpu.VMEM((LANES + SUM_ROWS, 2 * TQ_DA), F32),
        ],
        compiler_params=pltpu.CompilerParams(
            dimension_semantics=("arbitrary", "arbitrary"), vmem_limit_bytes=VMEM_LIMIT),
        name="diff_attn",
    )(scalars, lamv, subln_g, z3, z3, z3)


def _na_block_geometry(rows):
    kr = min(NA_ROWS_MAX, rows)
    nblk = rows // NA_QROWS
    geo = []
    for rb in range(nblk):
        ks = min(max(NA_QROWS * rb - kr // 2, 0), rows - NA_KROWS)
        var = 0 if rb == 0 else (2 if rb == nblk - 1 else 1)
        geo.append((ks, var))
    return geo


def _na_table_dr(rows):
    kr = min(NA_ROWS_MAX, rows)
    geo = _na_block_geometry(rows)
    rep = {0: 0, 1: 1, 2: len(geo) - 1}
    out = {}
    for var, rb in rep.items():
        ks = geo[rb][0]
        for ri in range(NA_QROWS):
            qr = NA_QROWS * rb + ri
            rs = min(max(qr - kr // 2, 0), rows - kr)
            for kj in range(NA_KROWS):
                krow = ks + kj
                out[(var, ri, kj)] = (krow - qr + NA_ROWS_MAX - 1) if rs <= krow < rs + kr else None
    return out


def _na_tables_kernel(rpb_ref, tab_ref, dict_ref, *, rows):
    row = pl.program_id(0) * NA_HEADS + pl.program_id(1)
    kc = lax.broadcasted_iota(jnp.int32, (GRID_W, GRID_W), 0)
    qc = lax.broadcasted_iota(jnp.int32, (GRID_W, GRID_W), 1)
    diff = kc - qc + (NA_COLS - 1)
    qcs = jnp.clip(qc - NA_COLS // 2, 0, GRID_W - NA_COLS)
    col_valid = (kc >= qcs) & (kc < qcs + NA_COLS)
    neg = jnp.full((GRID_W, GRID_W), -jnp.inf, F32)
    for dr in range(N_RPB_DR):
        t = jnp.zeros((GRID_W, GRID_W), F32)
        for dc in range(N_RPB_DC):
            t = jnp.where(diff == dc, rpb_ref[row, dr * N_RPB_DC + dc], t)
        dict_ref[dr] = jnp.where(col_valid, t * LOG2E, neg)
    for (var, ri, kj), dr in _na_table_dr(rows).items():
        tile = neg if dr is None else dict_ref[dr]
        tab_ref[0, 0, var, kj * GRID_W:(kj + 1) * GRID_W, ri * GRID_W:(ri + 1) * GRID_W] = tile


def _na_tables(rpb, rows):
    depth = rpb.shape[0]
    rpb2 = rpb.reshape(depth * NA_HEADS, N_RPB_DR * N_RPB_DC)
    return pl.pallas_call(
        functools.partial(_na_tables_kernel, rows=rows),
        grid=(depth, NA_HEADS),
        in_specs=[pl.BlockSpec(memory_space=pltpu.SMEM)],
        out_specs=pl.BlockSpec((1, 1, 3, NA_TK, NA_TQ), lambda l, h: (l, h // 2, 0, 0, h % 2)),
        out_shape=jax.ShapeDtypeStruct((depth, NA_HEADS // 2, 3, NA_TK, 2 * NA_TQ), F32),
        scratch_shapes=[pltpu.VMEM((N_RPB_DR, GRID_W, GRID_W), F32)],
        compiler_params=pltpu.CompilerParams(
            dimension_semantics=("arbitrary", "arbitrary"), vmem_limit_bytes=VMEM_LIMIT),
        name="na_tables",
    )(rpb2)


def _na_kernel(tab_ref, q_ref, k_ref, v_ref, o_ref, vt_ref, *, rows):
    lane = lax.broadcasted_iota(jnp.int32, (1, LANES), 1)
    vdim = lax.broadcasted_iota(jnp.int32, (LANES, 1), 0)
    nt = (((1,), (1,)), ((), ()))
    _fill_vt(vt_ref, v_ref)
    geo = _na_block_geometry(rows)

    def kslice(rb):
        return slice(geo[rb][0] * GRID_W, geo[rb][0] * GRID_W + NA_TK)

    def scores(rb):
        q = q_ref[0, rb * NA_TQ:(rb + 1) * NA_TQ, :]
        zero = jnp.zeros_like(q)
        qst = jnp.concatenate([jnp.where(lane < NA_HEAD_DIM, q, zero),
                               jnp.where(lane < NA_HEAD_DIM, zero, q)], axis=0)
        s = lax.dot_general(k_ref[0, kslice(rb), :], qst, nt, preferred_element_type=F32)
        return s + tab_ref[0, 0, geo[rb][1]]

    s = scores(0)
    for rb in range(len(geo)):
        s_next = scores(rb + 1) if rb + 1 < len(geo) else None
        e = jnp.exp2(s - jnp.max(s, axis=0, keepdims=True))
        pv = jnp.dot(vt_ref[:, kslice(rb)], e.astype(BF16), preferred_element_type=F32)
        on = pv[:LANES] * (1.0 / pv[LANES:LANES + 1])
        ot = jnp.where(vdim < NA_HEAD_DIM, on[:, :NA_TQ], on[:, NA_TQ:])
        o_ref[0, rb * NA_TQ:(rb + 1) * NA_TQ, :] = ot.T.astype(BF16)
        s = s_next


def _neigh_attention(z3, tables, layer):
    b, s, _ = z3.shape
    rows = s // GRID_W
    return pl.pallas_call(
        functools.partial(_na_kernel, rows=rows),
        grid=(NA_HEADS // 2, b),
        in_specs=[
            pl.BlockSpec((1, 1, 3, NA_TK, 2 * NA_TQ), lambda hp, i: (layer, hp, 0, 0, 0)),
            pl.BlockSpec((1, s, LANES), lambda hp, i: (i, 0, QN_BLK + hp)),
            pl.BlockSpec((1, s, LANES), lambda hp, i: (i, 0, KN_BLK + hp)),
            pl.BlockSpec((1, s, LANES), lambda hp, i: (i, 0, VN_BLK + hp)),
        ],
        out_specs=pl.BlockSpec((1, s, LANES), lambda hp, i: (i, 0, hp)),
        out_shape=jax.ShapeDtypeStruct((b, s, NA_HEADS * NA_HEAD_DIM), BF16),
        scratch_shapes=[pltpu.VMEM((LANES + SUM_ROWS, s), BF16)],
        compiler_params=pltpu.CompilerParams(
            dimension_semantics=("arbitrary", "arbitrary"), vmem_limit_bytes=VMEM_LIMIT),
        name="neigh_attn",
    )(tables, z3, z3, z3)


def _outproj_kernel(x_ref, oda_ref, ona_ref, gate_ref, w_ref, fg_ref, xo_ref, *, final):
    gate = gate_ref[...].astype(F32)
    o = jnp.concatenate([oda_ref[...], ona_ref[...]], axis=-1).astype(F32) * (gate * jax.nn.sigmoid(gate))
    xo = x_ref[...] + jnp.dot(o.astype(BF16), w_ref[...], preferred_element_type=F32)
    if final:
        xo = _rms(xo, fg_ref[...])
    xo_ref[...] = xo


def _outproj(x2, oda2, ona2, z2, w_bf16, final_g, final):
    m = x2.shape[0]
    half = D_MODEL // 2
    return pl.pallas_call(
        functools.partial(_outproj_kernel, final=final),
        grid=(m // TM_PROJ,),
        in_specs=[
            pl.BlockSpec((TM_PROJ, D_MODEL), lambda i: (i, 0)),
            pl.BlockSpec((TM_PROJ, half), lambda i: (i, 0)),
            pl.BlockSpec((TM_PROJ, half), lambda i: (i, 0)),
            pl.BlockSpec((TM_PROJ, D_MODEL), lambda i: (i, GATE_BLK)),
            pl.BlockSpec((D_MODEL, D_MODEL), lambda i: (0, 0)),
            pl.BlockSpec((1, D_MODEL), lambda i: (0, 0)),
        ],
        out_specs=pl.BlockSpec((TM_PROJ, D_MODEL), lambda i: (i, 0)),
        out_shape=jax.ShapeDtypeStruct((m, D_MODEL), F32),
        compiler_params=pltpu.CompilerParams(
            dimension_semantics=("arbitrary",), vmem_limit_bytes=VMEM_LIMIT),
        name="outproj_final" if final else "outproj",
    )(x2, oda2, ona2, z2, w_bf16, final_g)


def kernel(x, norm_g, w_in, w_out, lam_q1, lam_k1, lam_q2, lam_k2, subln_g, rpb, final_g):
    b, s, d = x.shape
    depth = norm_g.shape[0]
    assert d == D_MODEL and s % TQ_DA == 0 and s % (GRID_W * NA_QROWS) == 0
    rows = s // GRID_W
    assert rows >= NA_KROWS
    m = b * s
    w_in_bf = w_in.astype(BF16)
    w_out_bf = w_out.astype(BF16)
    slopes = np.asarray(2.0 ** (-8.0 * np.arange(1, DA_HEADS + 1) / DA_HEADS), dtype=np.float32)
    tables = _na_tables(rpb, rows)
    fg = final_g.reshape(1, d)
    x2 = x.reshape(m, d)
    for l in range(depth):
        lam_init = 0.8 - 0.6 * math.exp(-0.3 * l)
        scalars = jnp.asarray(np.concatenate(
            [slopes, np.asarray([lam_init, 1.0 - lam_init, 0.0, 0.0], np.float32)]))
        lamv = jnp.stack([lam_q1[l], lam_k1[l], lam_q2[l], lam_k2[l]])
        z2 = _inproj(x2, norm_g[l].reshape(1, d), w_in_bf[l])
        z3 = z2.reshape(b, s, IN_WIDTH)
        oda = _diff_attention(z3, scalars, lamv, subln_g[l].reshape(-1, 1))
        ona = _neigh_attention(z3, tables, l)
        x2 = _outproj(x2, oda.reshape(m, -1), ona.reshape(m, -1), z2, w_out_bf[l], fg,
                      final=(l == depth - 1))
    return x2.reshape(b, s, d)
```

```python
import functools
import math

import numpy as np
import jax
import jax.numpy as jnp
from jax import lax
from jax.experimental import pallas as pl
from jax.experimental.pallas import tpu as pltpu

F32 = jnp.float32
BF16 = jnp.bfloat16

D_MODEL = 1024
DEPTH = 4
GRID_W = 64
DA_HEADS = 4
DA_HEAD_DIM = 64
NA_HEADS = 8
NA_HEAD_DIM = 64
NA_ROWS_MAX = 8
NA_COLS = 16
RMS_EPS = 1e-6
IN_WIDTH = 4096
LANES = 128

QD_BLK, KD_BLK, VD_BLK = 0, 4, 8
QN_BLK, KN_BLK, VN_BLK = 12, 16, 20
GATE_BLK = 3

TM_PROJ = 512
TQ_DA = 256
TK_DA = TQ_DA
SUM_ROWS = 16
NA_QROWS = 4
NA_KROWS = 12
NA_TQ = NA_QROWS * GRID_W
NA_TK = NA_KROWS * GRID_W
N_RPB_DR = 2 * NA_ROWS_MAX - 1
N_RPB_DC = 2 * NA_COLS - 1
VMEM_LIMIT = 48 * 1024 * 1024
VMEM_LIMIT_BOUNDARY = 56 * 1024 * 1024
LOG2E = math.log2(math.e)
Q_SCALE = DA_HEAD_DIM ** -0.5 * LOG2E
assert NA_HEAD_DIM == DA_HEAD_DIM
Z_CHUNK = 512
Q_CHUNKS = (QD_BLK * LANES // Z_CHUNK, QN_BLK * LANES // Z_CHUNK)


def _rms(x, g):
    return x * lax.rsqrt(jnp.mean(x * x, axis=-1, keepdims=True) + RMS_EPS) * g


def _project_in(x, g_ref, w_ref, z_ref):
    h = _rms(x, g_ref[...]).astype(BF16)
    for j in range(IN_WIDTH // Z_CHUNK):
        cols = slice(j * Z_CHUNK, (j + 1) * Z_CHUNK)
        zc = jnp.dot(h, w_ref[:, cols], preferred_element_type=F32)
        if j in Q_CHUNKS:
            zc = zc * Q_SCALE
        z_ref[:, cols] = zc.astype(BF16)


def _inproj_kernel(x_ref, g_ref, w_ref, z_ref):
    _project_in(x_ref[...], g_ref, w_ref, z_ref)


def _inproj(x2, g, w_bf16):
    m = x2.shape[0]
    return pl.pallas_call(
        _inproj_kernel,
        grid=(m // TM_PROJ,),
        in_specs=[
            pl.BlockSpec((TM_PROJ, D_MODEL), lambda i: (i, 0)),
            pl.BlockSpec((1, D_MODEL), lambda i: (0, 0)),
            pl.BlockSpec((D_MODEL, IN_WIDTH), lambda i: (0, 0)),
        ],
        out_specs=pl.BlockSpec((TM_PROJ, IN_WIDTH), lambda i: (i, 0)),
        out_shape=jax.ShapeDtypeStruct((m, IN_WIDTH), BF16),
        compiler_params=pltpu.CompilerParams(
            dimension_semantics=("arbitrary",), vmem_limit_bytes=VMEM_LIMIT),
        name="inproj",
    )(x2, g, w_bf16)


def _fill_vt(vt_ref, v_ref):
    vt_ref[0:LANES, :] = v_ref[0].T
    vt_ref[LANES:, :] = jnp.ones((SUM_ROWS, vt_ref.shape[1]), BF16)


def _da_kernel(sc_ref, lamv_ref, g_ref, q_ref, k_ref, v_ref, o_ref,
               bias_ref, vt_ref, sa_ref, sb_ref, pa_ref, pb_ref, *, seq):
    nq = seq // TQ_DA
    nk = seq // TK_DA
    h = pl.program_id(0)
    lam_init = sc_ref[4]
    out_scale = sc_ref[5]

    @pl.when(pl.program_id(1) == 0)
    def _():
        slope = sc_ref[h] * LOG2E
        kr = lax.broadcasted_iota(jnp.int32, (TK_DA, TQ_DA), 0)
        qc = lax.broadcasted_iota(jnp.int32, (TK_DA, TQ_DA), 1)
        for t in range(nq + nk - 1):
            bias_ref[t] = slope * jnp.abs(kr - qc + (t - (nq - 1)) * TQ_DA).astype(F32)

    _fill_vt(vt_ref, v_ref)
    lv = lamv_ref[...]
    lam = (jnp.exp(jnp.sum(lv[0:1] * lv[1:2], axis=-1, keepdims=True))
           - jnp.exp(jnp.sum(lv[2:3] * lv[3:4], axis=-1, keepdims=True)) + lam_init)
    lane = lax.broadcasted_iota(jnp.int32, (1, LANES), 1)
    g = g_ref[...]
    nt = (((1,), (1,)), ((), ()))

    def rows_of(blk):
        if isinstance(blk, int):
            return slice(blk * TQ_DA, (blk + 1) * TQ_DA)
        return pl.ds(pl.multiple_of(blk * TQ_DA, TQ_DA), TQ_DA)

    def scores(blk, s_ref):
        q = q_ref[0, rows_of(blk), :]
        zero = jnp.zeros_like(q)
        qst = jnp.concatenate([jnp.where(lane < DA_HEAD_DIM, q, zero),
                               jnp.where(lane < DA_HEAD_DIM, zero, q)], axis=0)
        bias = jnp.concatenate([bias_ref[j - blk + nq - 1] for j in range(nk)], axis=0)
        s = lax.dot_general(k_ref[0], qst, nt, preferred_element_type=F32)
        s = s - jnp.concatenate([bias, bias], axis=1)
        s_ref[...] = s
        return jnp.max(s, axis=0, keepdims=True)

    def attend(s_ref, m, pv_ref):
        e = jnp.exp2(s_ref[...] - m)
        pv_ref[...] = jnp.dot(vt_ref[...], e.astype(BF16), preferred_element_type=F32)

    def finalize(blk, pv_ref):
        pv, l = pv_ref[0:LANES, :], pv_ref[LANES:LANES + 1, :]
        ot = pv[:, :TQ_DA] * (1.0 / l[:, :TQ_DA]) - pv[:, TQ_DA:] * (lam / l[:, TQ_DA:])
        ot = ot * lax.rsqrt(jnp.mean(ot * ot, axis=0, keepdims=True) + RMS_EPS) * g * out_scale
        o_ref[0, rows_of(blk), :] = ot.T.astype(BF16)

    def pair(p, m_even, first=False, last=False):
        if not first:
            finalize(2 * p - 1, pb_ref)
        m_odd = scores(2 * p + 1, sb_ref)
        attend(sa_ref, m_even, pa_ref)
        m_next = None if last else scores(2 * p + 2, sa_ref)
        finalize(2 * p, pa_ref)
        attend(sb_ref, m_odd, pb_ref)
        return m_next

    m_even = pair(0, scores(0, sa_ref), first=True)
    m_even = lax.fori_loop(1, nq // 2 - 1, pair, m_even)
    pair(nq // 2 - 1, m_even, last=True)
    finalize(nq - 1, pb_ref)


def _diff_attention(z3, scalars, lamv, subln_g):
    b, s, _ = z3.shape
    smem = pl.BlockSpec(memory_space=pltpu.SMEM)
    nq, nk = s // TQ_DA, s // TK_DA
    return pl.pallas_call(
        functools.partial(_da_kernel, seq=s),
        grid=(DA_HEADS, b),
        in_specs=[
            smem,
            pl.BlockSpec((4, DA_HEAD_DIM), lambda h, i: (0, 0)),
            pl.BlockSpec((2 * DA_HEAD_DIM, 1), lambda h, i: (0, 0)),
            pl.BlockSpec((1, s, LANES), lambda h, i: (i, 0, QD_BLK + h)),
            pl.BlockSpec((1, s, LANES), lambda h, i: (i, 0, KD_BLK + h)),
            pl.BlockSpec((1, s, LANES), lambda h, i: (i, 0, VD_BLK + h)),
        ],
        out_specs=pl.BlockSpec((1, s, LANES), lambda h, i: (i, 0, h)),
        out_shape=jax.ShapeDtypeStruct((b, s, DA_HEADS * LANES), BF16),
        scratch_shapes=[
            pltpu.VMEM((nq + nk - 1, TK_DA, TQ_DA), F32),
            pltpu.VMEM((LANES + SUM_ROWS, s), BF16),
            pltpu.VMEM((s, 2 * TQ_DA), F32),
            pltpu.VMEM((s, 2 * TQ_DA), F32),
            pltpu.VMEM((LANES + SUM_ROWS, 2 * TQ_DA), F32),
            pltpu.VMEM((LANES + SUM_ROWS, 2 * TQ_DA), F32),
        ],
        compiler_params=pltpu.CompilerParams(
            dimension_semantics=("arbitrary", "arbitrary"), vmem_limit_bytes=VMEM_LIMIT),
        name="diff_attn",
    )(scalars, lamv, subln_g, z3, z3, z3)


def _na_block_geometry(rows):
    kr = min(NA_ROWS_MAX, rows)
    nblk = rows // NA_QROWS
    geo = []
    for rb in range(nblk):
        ks = min(max(NA_QROWS * rb - kr // 2, 0), rows - NA_KROWS)
        var = 0 if rb == 0 else (2 if rb == nblk - 1 else 1)
        geo.append((ks, var))
    return geo


def _na_table_dr(rows):
    kr = min(NA_ROWS_MAX, rows)
    geo = _na_block_geometry(rows)
    rep = {0: 0, 1: 1, 2: len(geo) - 1}
    out = {}
    for var, rb in rep.items():
        ks = geo[rb][0]
        for ri in range(NA_QROWS):
            qr = NA_QROWS * rb + ri
            rs = min(max(qr - kr // 2, 0), rows - kr)
            for kj in range(NA_KROWS):
                krow = ks + kj
                out[(var, ri, kj)] = (krow - qr + NA_ROWS_MAX - 1) if rs <= krow < rs + kr else None
    return out


def _na_tables_kernel(rpb_ref, tab_ref, dict_ref, *, rows):
    row = pl.program_id(0) * NA_HEADS + pl.program_id(1)
    kc = lax.broadcasted_iota(jnp.int32, (GRID_W, GRID_W), 0)
    qc = lax.broadcasted_iota(jnp.int32, (GRID_W, GRID_W), 1)
    diff = kc - qc + (NA_COLS - 1)
    qcs = jnp.clip(qc - NA_COLS // 2, 0, GRID_W - NA_COLS)
    col_valid = (kc >= qcs) & (kc < qcs + NA_COLS)
    neg = jnp.full((GRID_W, GRID_W), -jnp.inf, F32)
    for dr in range(N_RPB_DR):
        t = jnp.zeros((GRID_W, GRID_W), F32)
        for dc in range(N_RPB_DC):
            t = jnp.where(diff == dc, rpb_ref[row, dr * N_RPB_DC + dc], t)
        dict_ref[dr] = jnp.where(col_valid, t * LOG2E, neg)
    for (var, ri, kj), dr in _na_table_dr(rows).items():
        tile = neg if dr is None else dict_ref[dr]
        tab_ref[0, 0, var, kj * GRID_W:(kj + 1) * GRID_W, ri * GRID_W:(ri + 1) * GRID_W] = tile


def _na_tables(rpb, rows):
    depth = rpb.shape[0]
    rpb2 = rpb.reshape(depth * NA_HEADS, N_RPB_DR * N_RPB_DC)
    return pl.pallas_call(
        functools.partial(_na_tables_kernel, rows=rows),
        grid=(depth, NA_HEADS),
        in_specs=[pl.BlockSpec(memory_space=pltpu.SMEM)],
        out_specs=pl.BlockSpec((1, 1, 3, NA_TK, NA_TQ), lambda l, h: (l, h // 2, 0, 0, h % 2)),
        out_shape=jax.ShapeDtypeStruct((depth, NA_HEADS // 2, 3, NA_TK, 2 * NA_TQ), F32),
        scratch_shapes=[pltpu.VMEM((N_RPB_DR, GRID_W, GRID_W), F32)],
        compiler_params=pltpu.CompilerParams(
            dimension_semantics=("arbitrary", "arbitrary"), vmem_limit_bytes=VMEM_LIMIT),
        name="na_tables",
    )(rpb2)


def _na_kernel(tab_ref, q_ref, k_ref, v_ref, o_ref, vt_ref, *, rows):
    lane = lax.broadcasted_iota(jnp.int32, (1, LANES), 1)
    vdim = lax.broadcasted_iota(jnp.int32, (LANES, 1), 0)
    nt = (((1,), (1,)), ((), ()))
    _fill_vt(vt_ref, v_ref)
    geo = _na_block_geometry(rows)

    def kslice(rb):
        return slice(geo[rb][0] * GRID_W, geo[rb][0] * GRID_W + NA_TK)

    def scores(rb):
        q = q_ref[0, rb * NA_TQ:(rb + 1) * NA_TQ, :]
        zero = jnp.zeros_like(q)
        qst = jnp.concatenate([jnp.where(lane < NA_HEAD_DIM, q, zero),
                               jnp.where(lane < NA_HEAD_DIM, zero, q)], axis=0)
        s = lax.dot_general(k_ref[0, kslice(rb), :], qst, nt, preferred_element_type=F32)
        return s + tab_ref[0, 0, geo[rb][1]]

    s = scores(0)
    for rb in range(len(geo)):
        s_next = scores(rb + 1) if rb + 1 < len(geo) else None
        e = jnp.exp2(s - jnp.max(s, axis=0, keepdims=True))
        pv = jnp.dot(vt_ref[:, kslice(rb)], e.astype(BF16), preferred_element_type=F32)
        on = pv[:LANES] * (1.0 / pv[LANES:LANES + 1])
        ot = jnp.where(vdim < NA_HEAD_DIM, on[:, :NA_TQ], on[:, NA_TQ:])
        o_ref[0, rb * NA_TQ:(rb + 1) * NA_TQ, :] = ot.T.astype(BF16)
        s = s_next


def _neigh_attention(z3, tables, layer):
    b, s, _ = z3.shape
    rows = s // GRID_W
    return pl.pallas_call(
        functools.partial(_na_kernel, rows=rows),
        grid=(NA_HEADS // 2, b),
        in_specs=[
            pl.BlockSpec((1, 1, 3, NA_TK, 2 * NA_TQ), lambda hp, i: (layer, hp, 0, 0, 0)),
            pl.BlockSpec((1, s, LANES), lambda hp, i: (i, 0, QN_BLK + hp)),
            pl.BlockSpec((1, s, LANES), lambda hp, i: (i, 0, KN_BLK + hp)),
            pl.BlockSpec((1, s, LANES), lambda hp, i: (i, 0, VN_BLK + hp)),
        ],
        out_specs=pl.BlockSpec((1, s, LANES), lambda hp, i: (i, 0, hp)),
        out_shape=jax.ShapeDtypeStruct((b, s, NA_HEADS * NA_HEAD_DIM), BF16),
        scratch_shapes=[pltpu.VMEM((LANES + SUM_ROWS, s), BF16)],
        compiler_params=pltpu.CompilerParams(
            dimension_semantics=("arbitrary", "arbitrary"), vmem_limit_bytes=VMEM_LIMIT),
        name="neigh_attn",
    )(tables, z3, z3, z3)


def _gated_out(x_ref, oda_ref, ona_ref, gate_ref, w_ref):
    gate = gate_ref[...].astype(F32)
    o = jnp.concatenate([oda_ref[...], ona_ref[...]], axis=-1).astype(F32) * (gate * jax.nn.sigmoid(gate))
    return x_ref[...] + jnp.dot(o.astype(BF16), w_ref[...], preferred_element_type=F32)


def _outproj_final_kernel(x_ref, oda_ref, ona_ref, gate_ref, w_ref, fg_ref, xo_ref):
    xo_ref[...] = _rms(_gated_out(x_ref, oda_ref, ona_ref, gate_ref, w_ref), fg_ref[...])


def _layer_boundary_kernel(x_ref, oda_ref, ona_ref, gate_ref, wout_ref, g_ref, win_ref, xo_ref, z_ref):
    xo = _gated_out(x_ref, oda_ref, ona_ref, gate_ref, wout_ref)
    xo_ref[...] = xo
    _project_in(xo, g_ref, win_ref, z_ref)


def _token_specs():
    half = D_MODEL // 2
    return [
        pl.BlockSpec((TM_PROJ, D_MODEL), lambda i: (i, 0)),
        pl.BlockSpec((TM_PROJ, half), lambda i: (i, 0)),
        pl.BlockSpec((TM_PROJ, half), lambda i: (i, 0)),
        pl.BlockSpec((TM_PROJ, D_MODEL), lambda i: (i, GATE_BLK)),
        pl.BlockSpec((D_MODEL, D_MODEL), lambda i: (0, 0)),
        pl.BlockSpec((1, D_MODEL), lambda i: (0, 0)),
    ]


def _outproj_final(x2, oda2, ona2, z2, w_bf16, final_g):
    m = x2.shape[0]
    return pl.pallas_call(
        _outproj_final_kernel,
        grid=(m // TM_PROJ,),
        in_specs=_token_specs(),
        out_specs=pl.BlockSpec((TM_PROJ, D_MODEL), lambda i: (i, 0)),
        out_shape=jax.ShapeDtypeStruct((m, D_MODEL), F32),
        compiler_params=pltpu.CompilerParams(
            dimension_semantics=("arbitrary",), vmem_limit_bytes=VMEM_LIMIT),
        name="outproj_final",
    )(x2, oda2, ona2, z2, w_bf16, final_g)


def _layer_boundary(x2, oda2, ona2, z2, wout_bf16, g_next, win_bf16):
    m = x2.shape[0]
    return pl.pallas_call(
        _layer_boundary_kernel,
        grid=(m // TM_PROJ,),
        in_specs=_token_specs() + [pl.BlockSpec((D_MODEL, IN_WIDTH), lambda i: (0, 0))],
        out_specs=[pl.BlockSpec((TM_PROJ, D_MODEL), lambda i: (i, 0)),
                   pl.BlockSpec((TM_PROJ, IN_WIDTH), lambda i: (i, 0))],
        out_shape=[jax.ShapeDtypeStruct((m, D_MODEL), F32),
                   jax.ShapeDtypeStruct((m, IN_WIDTH), BF16)],
        compiler_params=pltpu.CompilerParams(
            dimension_semantics=("arbitrary",), vmem_limit_bytes=VMEM_LIMIT_BOUNDARY),
        name="layer_boundary",
    )(x2, oda2, ona2, z2, wout_bf16, g_next, win_bf16)


def kernel(x, norm_g, w_in, w_out, lam_q1, lam_k1, lam_q2, lam_k2, subln_g, rpb, final_g):
    b, s, d = x.shape
    depth = norm_g.shape[0]
    assert d == D_MODEL and s % TQ_DA == 0 and s % (GRID_W * NA_QROWS) == 0
    rows = s // GRID_W
    assert rows >= NA_KROWS
    m = b * s
    w_in_bf = w_in.astype(BF16)
    w_out_bf = w_out.astype(BF16)
    slopes = np.asarray(2.0 ** (-8.0 * np.arange(1, DA_HEADS + 1) / DA_HEADS), dtype=np.float32)
    tables = _na_tables(rpb, rows)
    fg = final_g.reshape(1, d)
    x2 = x.reshape(m, d)
    z2 = _inproj(x2, norm_g[0].reshape(1, d), w_in_bf[0])
    for l in range(depth):
        lam_init = 0.8 - 0.6 * math.exp(-0.3 * l)
        scalars = jnp.asarray(np.concatenate(
            [slopes, np.asarray([lam_init, 1.0 - lam_init, 0.0, 0.0], np.float32)]))
        lamv = jnp.stack([lam_q1[l], lam_k1[l], lam_q2[l], lam_k2[l]])
        z3 = z2.reshape(b, s, IN_WIDTH)
        oda = _diff_attention(z3, scalars, lamv, subln_g[l].reshape(-1, 1)).reshape(m, -1)
        ona = _neigh_attention(z3, tables, l).reshape(m, -1)
        if l + 1 < depth:
            x2, z2 = _layer_boundary(x2, oda, ona, z2, w_out_bf[l],
                                     norm_g[l + 1].reshape(1, d), w_in_bf[l + 1])
        else:
            x2 = _outproj_final(x2, oda, ona, z2, w_out_bf[l], fg)
    return x2.reshape(b, s, d)
```

```python
import functools
import math

import numpy as np
import jax
import jax.numpy as jnp
from jax import lax
from jax.experimental import pallas as pl
from jax.experimental.pallas import tpu as pltpu

F32 = jnp.float32
BF16 = jnp.bfloat16

D_MODEL = 1024
DEPTH = 4
GRID_W = 64
DA_HEADS = 4
DA_HEAD_DIM = 64
NA_HEADS = 8
NA_HEAD_DIM = 64
NA_ROWS_MAX = 8
NA_COLS = 16
RMS_EPS = 1e-6
IN_WIDTH = 4096
LANES = 128

QD_BLK, KD_BLK, VD_BLK = 0, 4, 8
QN_BLK, KN_BLK, VN_BLK = 12, 16, 20
GATE_BLK = 3

TM_PROJ = 512
TQ_DA = 256
TK_DA = TQ_DA
SUM_ROWS = 16
NA_QROWS = 4
NA_KROWS = 12
NA_TQ = NA_QROWS * GRID_W
NA_TK = NA_KROWS * GRID_W
N_RPB_DR = 2 * NA_ROWS_MAX - 1
N_RPB_DC = 2 * NA_COLS - 1
VMEM_LIMIT = 48 * 1024 * 1024
VMEM_LIMIT_BOUNDARY = 56 * 1024 * 1024
LOG2E = math.log2(math.e)
Q_SCALE = DA_HEAD_DIM ** -0.5 * LOG2E
assert NA_HEAD_DIM == DA_HEAD_DIM
Z_CHUNK = 512
Q_CHUNKS = (QD_BLK * LANES // Z_CHUNK, QN_BLK * LANES // Z_CHUNK)
V_CHUNKS = (VD_BLK * LANES // Z_CHUNK, VN_BLK * LANES // Z_CHUNK)
VT_ROWS = LANES + SUM_ROWS
VT_GROUPS = len(V_CHUNKS) * Z_CHUNK // LANES
VT_NA_GROUP0 = Z_CHUNK // LANES


def _rms(x, g):
    return x * lax.rsqrt(jnp.mean(x * x, axis=-1, keepdims=True) + RMS_EPS) * g


def _project_in(x, g_ref, w_ref, z_ref, vt_ref):
    h = _rms(x, g_ref[...])
    ones = jnp.ones((SUM_ROWS, x.shape[0]), BF16)
    for j in range(IN_WIDTH // Z_CHUNK):
        cols = slice(j * Z_CHUNK, (j + 1) * Z_CHUNK)
        zc = jnp.dot(h, w_ref[:, cols], preferred_element_type=F32)
        if j in Q_CHUNKS:
            zc = zc * Q_SCALE
        z_ref[:, cols] = zc.astype(BF16)
        if j in V_CHUNKS:
            zt = zc.T.astype(BF16)
            for t in range(Z_CHUNK // LANES):
                grp = V_CHUNKS.index(j) * (Z_CHUNK // LANES) + t
                vt_ref[grp * VT_ROWS:grp * VT_ROWS + LANES, :] = zt[t * LANES:(t + 1) * LANES]
                vt_ref[grp * VT_ROWS + LANES:(grp + 1) * VT_ROWS, :] = ones


def _inproj_kernel(x_ref, g_ref, w_ref, z_ref, vt_ref):
    _project_in(x_ref[...], g_ref, w_ref, z_ref, vt_ref)


def _resident(shape, layer):
    return pl.BlockSpec((None,) + shape, lambda i: (layer,) + (0,) * len(shape),
                        pipeline_mode=pl.Buffered(1))


def _inproj(x2, norm_g, w_in, layer):
    m = x2.shape[0]
    return pl.pallas_call(
        _inproj_kernel,
        grid=(m // TM_PROJ,),
        in_specs=[
            pl.BlockSpec((TM_PROJ, D_MODEL), lambda i: (i, 0)),
            _resident((1, D_MODEL), layer),
            _resident((D_MODEL, IN_WIDTH), layer),
        ],
        out_specs=_z_out_specs(),
        out_shape=_z_out_shapes(m),
        compiler_params=pltpu.CompilerParams(
            dimension_semantics=("arbitrary",), vmem_limit_bytes=VMEM_LIMIT),
        name="inproj",
    )(x2, norm_g, w_in)


def _z_out_specs():
    return [pl.BlockSpec((TM_PROJ, IN_WIDTH), lambda i: (i, 0)),
            pl.BlockSpec((VT_GROUPS * VT_ROWS, TM_PROJ), lambda i: (0, i))]


def _z_out_shapes(m):
    return [jax.ShapeDtypeStruct((m, IN_WIDTH), BF16),
            jax.ShapeDtypeStruct((VT_GROUPS * VT_ROWS, m), BF16)]


def _da_kernel(sc_ref, lamv_ref, g_ref, q_ref, k_ref, vt_ref, o_ref,
               bias_ref, sa_ref, sb_ref, pa_ref, pb_ref, *, seq):
    nq = seq // TQ_DA
    nk = seq // TK_DA
    h = pl.program_id(0)
    lam_init = sc_ref[4]
    out_scale = sc_ref[5]

    @pl.when(pl.program_id(1) == 0)
    def _():
        slope = sc_ref[h] * LOG2E
        kr = lax.broadcasted_iota(jnp.int32, (TK_DA, TQ_DA), 0)
        qc = lax.broadcasted_iota(jnp.int32, (TK_DA, TQ_DA), 1)
        for t in range(nq + nk - 1):
            bias_ref[t] = slope * jnp.abs(kr - qc + (t - (nq - 1)) * TQ_DA).astype(F32)

    lv = lamv_ref[...]
    lam = (jnp.exp(jnp.sum(lv[0:1] * lv[1:2], axis=-1, keepdims=True))
           - jnp.exp(jnp.sum(lv[2:3] * lv[3:4], axis=-1, keepdims=True)) + lam_init)
    lane = lax.broadcasted_iota(jnp.int32, (1, LANES), 1)
    g = g_ref[...]
    nt = (((1,), (1,)), ((), ()))

    def rows_of(blk):
        if isinstance(blk, int):
            return slice(blk * TQ_DA, (blk + 1) * TQ_DA)
        return pl.ds(pl.multiple_of(blk * TQ_DA, TQ_DA), TQ_DA)

    def scores(blk, s_ref):
        q = q_ref[0, rows_of(blk), :]
        zero = jnp.zeros_like(q)
        qst = jnp.concatenate([jnp.where(lane < DA_HEAD_DIM, q, zero),
                               jnp.where(lane < DA_HEAD_DIM, zero, q)], axis=0)
        bias = jnp.concatenate([bias_ref[j - blk + nq - 1] for j in range(nk)], axis=0)
        s = lax.dot_general(k_ref[0], qst, nt, preferred_element_type=F32)
        s = s - jnp.concatenate([bias, bias], axis=1)
        s_ref[...] = s
        return jnp.max(s, axis=0, keepdims=True)

    def attend(s_ref, m, pv_ref):
        e = jnp.exp2(s_ref[...] - m)
        pv_ref[...] = jnp.dot(vt_ref[...], e.astype(BF16), preferred_element_type=F32)

    def finalize(blk, pv_ref):
        pv, l = pv_ref[0:LANES, :], pv_ref[LANES:LANES + 1, :]
        ot = pv[:, :TQ_DA] * (1.0 / l[:, :TQ_DA]) - pv[:, TQ_DA:] * (lam / l[:, TQ_DA:])
        ot = ot * lax.rsqrt(jnp.mean(ot * ot, axis=0, keepdims=True) + RMS_EPS) * g * out_scale
        o_ref[0, rows_of(blk), :] = ot.T.astype(BF16)

    def pair(p, m_even, first=False, last=False):
        if not first:
            finalize(2 * p - 1, pb_ref)
        m_odd = scores(2 * p + 1, sb_ref)
        attend(sa_ref, m_even, pa_ref)
        m_next = None if last else scores(2 * p + 2, sa_ref)
        finalize(2 * p, pa_ref)
        attend(sb_ref, m_odd, pb_ref)
        return m_next

    m_even = pair(0, scores(0, sa_ref), first=True)
    m_even = lax.fori_loop(1, nq // 2 - 1, pair, m_even)
    pair(nq // 2 - 1, m_even, last=True)
    finalize(nq - 1, pb_ref)


def _diff_attention(z3, vt, scalars, lamv, subln_g):
    b, s, _ = z3.shape
    smem = pl.BlockSpec(memory_space=pltpu.SMEM)
    nq, nk = s // TQ_DA, s // TK_DA
    return pl.pallas_call(
        functools.partial(_da_kernel, seq=s),
        grid=(DA_HEADS, b),
        in_specs=[
            smem,
            pl.BlockSpec((4, DA_HEAD_DIM), lambda h, i: (0, 0)),
            pl.BlockSpec((2 * DA_HEAD_DIM, 1), lambda h, i: (0, 0)),
            pl.BlockSpec((1, s, LANES), lambda h, i: (i, 0, QD_BLK + h)),
            pl.BlockSpec((1, s, LANES), lambda h, i: (i, 0, KD_BLK + h)),
            pl.BlockSpec((VT_ROWS, s), lambda h, i: (h, i)),
        ],
        out_specs=pl.BlockSpec((1, s, LANES), lambda h, i: (i, 0, h)),
        out_shape=jax.ShapeDtypeStruct((b, s, DA_HEADS * LANES), BF16),
        scratch_shapes=[
            pltpu.VMEM((nq + nk - 1, TK_DA, TQ_DA), F32),
            pltpu.VMEM((s, 2 * TQ_DA), F32),
            pltpu.VMEM((s, 2 * TQ_DA), F32),
            pltpu.VMEM((LANES + SUM_ROWS, 2 * TQ_DA), F32),
            pltpu.VMEM((LANES + SUM_ROWS, 2 * TQ_DA), F32),
        ],
        compiler_params=pltpu.CompilerParams(
            dimension_semantics=("arbitrary", "arbitrary"), vmem_limit_bytes=VMEM_LIMIT),
        name="diff_attn",
    )(scalars, lamv, subln_g, z3, z3, vt)


def _na_block_geometry(rows):
    kr = min(NA_ROWS_MAX, rows)
    nblk = rows // NA_QROWS
    geo = []
    for rb in range(nblk):
        ks = min(max(NA_QROWS * rb - kr // 2, 0), rows - NA_KROWS)
        var = 0 if rb == 0 else (2 if rb == nblk - 1 else 1)
        geo.append((ks, var))
    return geo


def _na_table_dr(rows):
    kr = min(NA_ROWS_MAX, rows)
    geo = _na_block_geometry(rows)
    rep = {0: 0, 1: 1, 2: len(geo) - 1}
    out = {}
    for var, rb in rep.items():
        ks = geo[rb][0]
        for ri in range(NA_QROWS):
            qr = NA_QROWS * rb + ri
            rs = min(max(qr - kr // 2, 0), rows - kr)
            for kj in range(NA_KROWS):
                krow = ks + kj
                out[(var, ri, kj)] = (krow - qr + NA_ROWS_MAX - 1) if rs <= krow < rs + kr else None
    return out


def _na_tables_kernel(rpb_ref, tab_ref, dict_ref, *, rows):
    row = pl.program_id(0) * NA_HEADS + pl.program_id(1)
    kc = lax.broadcasted_iota(jnp.int32, (GRID_W, GRID_W), 0)
    qc = lax.broadcasted_iota(jnp.int32, (GRID_W, GRID_W), 1)
    diff = kc - qc + (NA_COLS - 1)
    qcs = jnp.clip(qc - NA_COLS // 2, 0, GRID_W - NA_COLS)
    col_valid = (kc >= qcs) & (kc < qcs + NA_COLS)
    neg = jnp.full((GRID_W, GRID_W), -jnp.inf, F32)
    for dr in range(N_RPB_DR):
        t = jnp.zeros((GRID_W, GRID_W), F32)
        for dc in range(N_RPB_DC):
            t = jnp.where(diff == dc, rpb_ref[row, dr * N_RPB_DC + dc], t)
        dict_ref[dr] = jnp.where(col_valid, t * LOG2E, neg)
    for (var, ri, kj), dr in _na_table_dr(rows).items():
        tile = neg if dr is None else dict_ref[dr]
        tab_ref[0, 0, var, kj * GRID_W:(kj + 1) * GRID_W, ri * GRID_W:(ri + 1) * GRID_W] = tile


def _na_tables(rpb, rows):
    depth = rpb.shape[0]
    rpb2 = rpb.reshape(depth * NA_HEADS, N_RPB_DR * N_RPB_DC)
    return pl.pallas_call(
        functools.partial(_na_tables_kernel, rows=rows),
        grid=(depth, NA_HEADS),
        in_specs=[pl.BlockSpec(memory_space=pltpu.SMEM)],
        out_specs=pl.BlockSpec((1, 1, 3, NA_TK, NA_TQ), lambda l, h: (l, h // 2, 0, 0, h % 2)),
        out_shape=jax.ShapeDtypeStruct((depth, NA_HEADS // 2, 3, NA_TK, 2 * NA_TQ), F32),
        scratch_shapes=[pltpu.VMEM((N_RPB_DR, GRID_W, GRID_W), F32)],
        compiler_params=pltpu.CompilerParams(
            dimension_semantics=("arbitrary", "arbitrary"), vmem_limit_bytes=VMEM_LIMIT),
        name="na_tables",
    )(rpb2)


def _na_kernel(tab_ref, q_ref, k_ref, vt_ref, o_ref, *, rows):
    lane = lax.broadcasted_iota(jnp.int32, (1, LANES), 1)
    vdim = lax.broadcasted_iota(jnp.int32, (LANES, 1), 0)
    nt = (((1,), (1,)), ((), ()))
    geo = _na_block_geometry(rows)

    def kslice(rb):
        return slice(geo[rb][0] * GRID_W, geo[rb][0] * GRID_W + NA_TK)

    def scores(rb):
        q = q_ref[0, rb * NA_TQ:(rb + 1) * NA_TQ, :]
        zero = jnp.zeros_like(q)
        qst = jnp.concatenate([jnp.where(lane < NA_HEAD_DIM, q, zero),
                               jnp.where(lane < NA_HEAD_DIM, zero, q)], axis=0)
        s = lax.dot_general(k_ref[0, kslice(rb), :], qst, nt, preferred_element_type=F32)
        return s + tab_ref[0, 0, geo[rb][1]]

    s = scores(0)
    for rb in range(len(geo)):
        s_next = scores(rb + 1) if rb + 1 < len(geo) else None
        e = jnp.exp2(s - jnp.max(s, axis=0, keepdims=True))
        pv = jnp.dot(vt_ref[:, kslice(rb)], e.astype(BF16), preferred_element_type=F32)
        on = pv[:LANES] * (1.0 / pv[LANES:LANES + 1])
        ot = jnp.where(vdim < NA_HEAD_DIM, on[:, :NA_TQ], on[:, NA_TQ:])
        o_ref[0, rb * NA_TQ:(rb + 1) * NA_TQ, :] = ot.T.astype(BF16)
        s = s_next


def _neigh_attention(z3, vt, tables, layer):
    b, s, _ = z3.shape
    rows = s // GRID_W
    return pl.pallas_call(
        functools.partial(_na_kernel, rows=rows),
        grid=(NA_HEADS // 2, b),
        in_specs=[
            pl.BlockSpec((1, 1, 3, NA_TK, 2 * NA_TQ), lambda hp, i: (layer, hp, 0, 0, 0)),
            pl.BlockSpec((1, s, LANES), lambda hp, i: (i, 0, QN_BLK + hp)),
            pl.BlockSpec((1, s, LANES), lambda hp, i: (i, 0, KN_BLK + hp)),
            pl.BlockSpec((VT_ROWS, s), lambda hp, i: (VT_NA_GROUP0 + hp, i)),
        ],
        out_specs=pl.BlockSpec((1, s, LANES), lambda hp, i: (i, 0, hp)),
        out_shape=jax.ShapeDtypeStruct((b, s, NA_HEADS * NA_HEAD_DIM), BF16),
        compiler_params=pltpu.CompilerParams(
            dimension_semantics=("arbitrary", "arbitrary"), vmem_limit_bytes=VMEM_LIMIT),
        name="neigh_attn",
    )(tables, z3, z3, vt)


def _gated_out(x_ref, oda_ref, ona_ref, gate_ref, w_ref):
    gate = gate_ref[...].astype(F32)
    o = jnp.concatenate([oda_ref[...], ona_ref[...]], axis=-1).astype(F32) * (gate * jax.nn.sigmoid(gate))
    return x_ref[...] + jnp.dot(o, w_ref[...], preferred_element_type=F32)


def _outproj_final_kernel(x_ref, oda_ref, ona_ref, gate_ref, w_ref, fg_ref, xo_ref):
    xo_ref[...] = _rms(_gated_out(x_ref, oda_ref, ona_ref, gate_ref, w_ref), fg_ref[...])


def _layer_boundary_kernel(x_ref, oda_ref, ona_ref, gate_ref, wout_ref, g_ref, win_ref,
                           xo_ref, z_ref, vt_ref):
    xo = _gated_out(x_ref, oda_ref, ona_ref, gate_ref, wout_ref)
    xo_ref[...] = xo
    _project_in(xo, g_ref, win_ref, z_ref, vt_ref)


def _token_specs(layer):
    half = D_MODEL // 2
    return [
        pl.BlockSpec((TM_PROJ, D_MODEL), lambda i: (i, 0)),
        pl.BlockSpec((TM_PROJ, half), lambda i: (i, 0)),
        pl.BlockSpec((TM_PROJ, half), lambda i: (i, 0)),
        pl.BlockSpec((TM_PROJ, D_MODEL), lambda i: (i, GATE_BLK)),
        _resident((D_MODEL, D_MODEL), layer),
    ]


def _outproj_final(x2, oda2, ona2, z2, w_out, layer, final_g):
    m = x2.shape[0]
    return pl.pallas_call(
        _outproj_final_kernel,
        grid=(m // TM_PROJ,),
        in_specs=_token_specs(layer) + [pl.BlockSpec((1, D_MODEL), lambda i: (0, 0))],
        out_specs=pl.BlockSpec((TM_PROJ, D_MODEL), lambda i: (i, 0)),
        out_shape=jax.ShapeDtypeStruct((m, D_MODEL), F32),
        compiler_params=pltpu.CompilerParams(
            dimension_semantics=("arbitrary",), vmem_limit_bytes=VMEM_LIMIT),
        name="outproj_final",
    )(x2, oda2, ona2, z2, w_out, final_g)


def _layer_boundary(x2, oda2, ona2, z2, w_out, norm_g, w_in, layer):
    m = x2.shape[0]
    return pl.pallas_call(
        _layer_boundary_kernel,
        grid=(m // TM_PROJ,),
        in_specs=_token_specs(layer) + [_resident((1, D_MODEL), layer + 1),
                                        _resident((D_MODEL, IN_WIDTH), layer + 1)],
        out_specs=[pl.BlockSpec((TM_PROJ, D_MODEL), lambda i: (i, 0))] + _z_out_specs(),
        out_shape=[jax.ShapeDtypeStruct((m, D_MODEL), F32)] + _z_out_shapes(m),
        compiler_params=pltpu.CompilerParams(
            dimension_semantics=("arbitrary",), vmem_limit_bytes=VMEM_LIMIT_BOUNDARY),
        name="layer_boundary",
    )(x2, oda2, ona2, z2, w_out, norm_g, w_in)


def kernel(x, norm_g, w_in, w_out, lam_q1, lam_k1, lam_q2, lam_k2, subln_g, rpb, final_g):
    b, s, d = x.shape
    depth = norm_g.shape[0]
    assert d == D_MODEL and s % TQ_DA == 0 and s % (GRID_W * NA_QROWS) == 0
    rows = s // GRID_W
    assert rows >= NA_KROWS
    m = b * s
    slopes = np.asarray(2.0 ** (-8.0 * np.arange(1, DA_HEADS + 1) / DA_HEADS), dtype=np.float32)
    tables = _na_tables(rpb, rows)
    fg = final_g.reshape(1, d)
    norm_g3 = norm_g.reshape(depth, 1, d)
    x2 = x.reshape(m, d)
    z2, vt = _inproj(x2, norm_g3, w_in, 0)
    for l in range(depth):
        lam_init = 0.8 - 0.6 * math.exp(-0.3 * l)
        scalars = jnp.asarray(np.concatenate(
            [slopes, np.asarray([lam_init, 1.0 - lam_init, 0.0, 0.0], np.float32)]))
        lamv = jnp.stack([lam_q1[l], lam_k1[l], lam_q2[l], lam_k2[l]])
        z3 = z2.reshape(b, s, IN_WIDTH)
        oda = _diff_attention(z3, vt, scalars, lamv, subln_g[l].reshape(-1, 1)).reshape(m, -1)
        ona = _neigh_attention(z3, vt, tables, l).reshape(m, -1)
        if l + 1 < depth:
            x2, z2, vt = _layer_boundary(x2, oda, ona, z2, w_out, norm_g3, w_in, l)
        else:
            x2 = _outproj_final(x2, oda, ona, z2, w_out, l, fg)
    return x2.reshape(b, s, d)
```

```python
import functools
import math

import numpy as np
import jax
import jax.numpy as jnp
from jax import lax
from jax.experimental import pallas as pl
from jax.experimental.pallas import tpu as pltpu

F32 = jnp.float32
BF16 = jnp.bfloat16

D_MODEL = 1024
DEPTH = 4
GRID_W = 64
DA_HEADS = 4
DA_HEAD_DIM = 64
NA_HEADS = 8
NA_HEAD_DIM = 64
NA_ROWS_MAX = 8
NA_COLS = 16
RMS_EPS = 1e-6
IN_WIDTH = 4096
LANES = 128

QD_BLK, KD_BLK, VD_BLK = 0, 4, 8
QN_BLK, KN_BLK, VN_BLK = 12, 16, 20
GATE_BLK = 3

TM_PROJ = 512
TQ_DA = 256
TK_DA = TQ_DA
SUM_ROWS = 16
NA_QROWS = 4
NA_KROWS = 12
NA_MAX_MERGE = 2
NA_TQ = NA_QROWS * GRID_W
NA_TK = NA_KROWS * GRID_W
N_RPB_DR = 2 * NA_ROWS_MAX - 1
N_RPB_DC = 2 * NA_COLS - 1
VMEM_LIMIT = 48 * 1024 * 1024
VMEM_LIMIT_BOUNDARY = 56 * 1024 * 1024
LOG2E = math.log2(math.e)
Q_SCALE = DA_HEAD_DIM ** -0.5 * LOG2E
assert NA_HEAD_DIM == DA_HEAD_DIM
Z_CHUNK = 512
Q_CHUNKS = (QD_BLK * LANES // Z_CHUNK, QN_BLK * LANES // Z_CHUNK)
V_CHUNKS = (VD_BLK * LANES // Z_CHUNK, VN_BLK * LANES // Z_CHUNK)
VT_ROWS = LANES + SUM_ROWS
VT_GROUPS = len(V_CHUNKS) * Z_CHUNK // LANES
VT_NA_GROUP0 = Z_CHUNK // LANES


def _rms(x, g):
    return x * lax.rsqrt(jnp.mean(x * x, axis=-1, keepdims=True) + RMS_EPS) * g


def _project_in(x, g_ref, w_ref, z_ref, vt_ref):
    h = _rms(x, g_ref[...])
    ones = jnp.ones((SUM_ROWS, x.shape[0]), BF16)
    for j in range(IN_WIDTH // Z_CHUNK):
        cols = slice(j * Z_CHUNK, (j + 1) * Z_CHUNK)
        zc = jnp.dot(h, w_ref[:, cols], preferred_element_type=F32)
        if j in Q_CHUNKS:
            zc = zc * Q_SCALE
        z_ref[:, cols] = zc.astype(BF16)
        if j in V_CHUNKS:
            zt = zc.T.astype(BF16)
            for t in range(Z_CHUNK // LANES):
                grp = V_CHUNKS.index(j) * (Z_CHUNK // LANES) + t
                vt_ref[grp * VT_ROWS:grp * VT_ROWS + LANES, :] = zt[t * LANES:(t + 1) * LANES]
                vt_ref[grp * VT_ROWS + LANES:(grp + 1) * VT_ROWS, :] = ones


def _inproj_kernel(x_ref, g_ref, w_ref, z_ref, vt_ref):
    _project_in(x_ref[...], g_ref, w_ref, z_ref, vt_ref)


def _resident(shape, layer):
    return pl.BlockSpec((None,) + shape, lambda i: (layer,) + (0,) * len(shape),
                        pipeline_mode=pl.Buffered(1))


def _inproj(x2, norm_g, w_in, layer):
    m = x2.shape[0]
    return pl.pallas_call(
        _inproj_kernel,
        grid=(m // TM_PROJ,),
        in_specs=[
            pl.BlockSpec((TM_PROJ, D_MODEL), lambda i: (i, 0)),
            _resident((1, D_MODEL), layer),
            _resident((D_MODEL, IN_WIDTH), layer),
        ],
        out_specs=_z_out_specs(),
        out_shape=_z_out_shapes(m),
        compiler_params=pltpu.CompilerParams(
            dimension_semantics=("arbitrary",), vmem_limit_bytes=VMEM_LIMIT),
        name="inproj",
    )(x2, norm_g, w_in)


def _z_out_specs():
    return [pl.BlockSpec((TM_PROJ, IN_WIDTH), lambda i: (i, 0)),
            pl.BlockSpec((VT_GROUPS * VT_ROWS, TM_PROJ), lambda i: (0, i))]


def _z_out_shapes(m):
    return [jax.ShapeDtypeStruct((m, IN_WIDTH), BF16),
            jax.ShapeDtypeStruct((VT_GROUPS * VT_ROWS, m), BF16)]


def _da_kernel(sc_ref, lamv_ref, g_ref, q_ref, k_ref, vt_ref, o_ref,
               bias_ref, sa_ref, sb_ref, pa_ref, pb_ref, *, seq):
    nq = seq // TQ_DA
    nk = seq // TK_DA
    h = pl.program_id(0)
    lam_init = sc_ref[4]
    out_scale = sc_ref[5]

    @pl.when(pl.program_id(1) == 0)
    def _():
        slope = sc_ref[h] * LOG2E
        kr = lax.broadcasted_iota(jnp.int32, (TK_DA, TQ_DA), 0)
        qc = lax.broadcasted_iota(jnp.int32, (TK_DA, TQ_DA), 1)
        for t in range(nq + nk - 1):
            bias_ref[t] = slope * jnp.abs(kr - qc + (t - (nq - 1)) * TQ_DA).astype(F32)

    lv = lamv_ref[...]
    lam = (jnp.exp(jnp.sum(lv[0:1] * lv[1:2], axis=-1, keepdims=True))
           - jnp.exp(jnp.sum(lv[2:3] * lv[3:4], axis=-1, keepdims=True)) + lam_init)
    lane = lax.broadcasted_iota(jnp.int32, (1, LANES), 1)
    g = g_ref[...]
    row0 = pl.multiple_of(jnp.minimum(pl.program_id(1), 0), 8)
    nt = (((1,), (1,)), ((), ()))

    def rows_of(blk):
        if isinstance(blk, int):
            return slice(blk * TQ_DA, (blk + 1) * TQ_DA)
        return pl.ds(pl.multiple_of(blk * TQ_DA, TQ_DA), TQ_DA)

    def scores(blk, s_ref):
        q = q_ref[0, rows_of(blk), :]
        zero = jnp.zeros_like(q)
        qst = jnp.concatenate([jnp.where(lane < DA_HEAD_DIM, q, zero),
                               jnp.where(lane < DA_HEAD_DIM, zero, q)], axis=0)
        bias = jnp.concatenate([bias_ref[j - blk + nq - 1] for j in range(nk)], axis=0)
        s = lax.dot_general(k_ref[0], qst, nt, preferred_element_type=F32)
        s = s - jnp.concatenate([bias, bias], axis=1)
        s_ref[...] = s
        return jnp.max(s, axis=0, keepdims=True)

    def attend(s_ref, m, pv_ref):
        e = jnp.exp2(s_ref[pl.ds(row0, seq), :] - m)
        pv_ref[...] = jnp.dot(vt_ref[...], e.astype(BF16), preferred_element_type=F32)

    def finalize(blk, pv_ref):
        pv, l = pv_ref[0:LANES, :], pv_ref[LANES:LANES + 1, :]
        ot = pv[:, :TQ_DA] * (1.0 / l[:, :TQ_DA]) - pv[:, TQ_DA:] * (lam / l[:, TQ_DA:])
        ot = ot * lax.rsqrt(jnp.mean(ot * ot, axis=0, keepdims=True) + RMS_EPS) * g * out_scale
        o_ref[0, rows_of(blk), :] = ot.T.astype(BF16)

    def pair(p, m_even, first=False, last=False):
        if not first:
            finalize(2 * p - 1, pb_ref)
        m_odd = scores(2 * p + 1, sb_ref)
        attend(sa_ref, m_even, pa_ref)
        m_next = None if last else scores(2 * p + 2, sa_ref)
        finalize(2 * p, pa_ref)
        attend(sb_ref, m_odd, pb_ref)
        return m_next

    m_even = pair(0, scores(0, sa_ref), first=True)
    for p in range(1, nq // 2 - 1):
        m_even = pair(p, m_even)
    pair(nq // 2 - 1, m_even, last=True)
    finalize(nq - 1, pb_ref)


def _diff_attention(z3, vt, scalars, lamv, subln_g):
    b, s, _ = z3.shape
    smem = pl.BlockSpec(memory_space=pltpu.SMEM)
    nq, nk = s // TQ_DA, s // TK_DA
    return pl.pallas_call(
        functools.partial(_da_kernel, seq=s),
        grid=(DA_HEADS, b),
        in_specs=[
            smem,
            pl.BlockSpec((4, DA_HEAD_DIM), lambda h, i: (0, 0)),
            pl.BlockSpec((2 * DA_HEAD_DIM, 1), lambda h, i: (0, 0)),
            pl.BlockSpec((1, s, LANES), lambda h, i: (i, 0, QD_BLK + h)),
            pl.BlockSpec((1, s, LANES), lambda h, i: (i, 0, KD_BLK + h)),
            pl.BlockSpec((VT_ROWS, s), lambda h, i: (h, i)),
        ],
        out_specs=pl.BlockSpec((1, s, LANES), lambda h, i: (i, 0, h)),
        out_shape=jax.ShapeDtypeStruct((b, s, DA_HEADS * LANES), BF16),
        scratch_shapes=[
            pltpu.VMEM((nq + nk - 1, TK_DA, TQ_DA), F32),
            pltpu.VMEM((s, 2 * TQ_DA), F32),
            pltpu.VMEM((s, 2 * TQ_DA), F32),
            pltpu.VMEM((LANES + SUM_ROWS, 2 * TQ_DA), F32),
            pltpu.VMEM((LANES + SUM_ROWS, 2 * TQ_DA), F32),
        ],
        compiler_params=pltpu.CompilerParams(
            dimension_semantics=("arbitrary", "arbitrary"), vmem_limit_bytes=VMEM_LIMIT),
        name="diff_attn",
    )(scalars, lamv, subln_g, z3, z3, vt)


def _na_block_geometry(rows):
    kr = min(NA_ROWS_MAX, rows)
    nblk = rows // NA_QROWS
    geo = []
    for rb in range(nblk):
        ks = min(max(NA_QROWS * rb - kr // 2, 0), rows - NA_KROWS)
        var = 0 if rb == 0 else (2 if rb == nblk - 1 else 1)
        geo.append((ks, var))
    return geo


def _na_table_dr(rows):
    kr = min(NA_ROWS_MAX, rows)
    geo = _na_block_geometry(rows)
    rep = {0: 0, 1: 1, 2: len(geo) - 1}
    out = {}
    for var, rb in rep.items():
        ks = geo[rb][0]
        for ri in range(NA_QROWS):
            qr = NA_QROWS * rb + ri
            rs = min(max(qr - kr // 2, 0), rows - kr)
            for kj in range(NA_KROWS):
                krow = ks + kj
                out[(var, ri, kj)] = (krow - qr + NA_ROWS_MAX - 1) if rs <= krow < rs + kr else None
    return out


def _na_tables_kernel(rpb_ref, tab_ref, dict_ref, *, rows):
    row = pl.program_id(0) * NA_HEADS + pl.program_id(1)
    kc = lax.broadcasted_iota(jnp.int32, (GRID_W, GRID_W), 0)
    qc = lax.broadcasted_iota(jnp.int32, (GRID_W, GRID_W), 1)
    diff = kc - qc + (NA_COLS - 1)
    qcs = jnp.clip(qc - NA_COLS // 2, 0, GRID_W - NA_COLS)
    col_valid = (kc >= qcs) & (kc < qcs + NA_COLS)
    neg = jnp.full((GRID_W, GRID_W), -jnp.inf, F32)
    for dr in range(N_RPB_DR):
        t = jnp.zeros((GRID_W, GRID_W), F32)
        for dc in range(N_RPB_DC):
            t = jnp.where(diff == dc, rpb_ref[row, dr * N_RPB_DC + dc], t)
        dict_ref[dr] = jnp.where(col_valid, t * LOG2E, neg)
    for (var, ri, kj), dr in _na_table_dr(rows).items():
        tile = neg if dr is None else dict_ref[dr]
        tab_ref[0, 0, var, kj * GRID_W:(kj + 1) * GRID_W, ri * GRID_W:(ri + 1) * GRID_W] = tile


def _na_tables(rpb, rows):
    depth = rpb.shape[0]
    rpb2 = rpb.reshape(depth * NA_HEADS, N_RPB_DR * N_RPB_DC)
    return pl.pallas_call(
        functools.partial(_na_tables_kernel, rows=rows),
        grid=(depth, NA_HEADS),
        in_specs=[pl.BlockSpec(memory_space=pltpu.SMEM)],
        out_specs=pl.BlockSpec((1, 1, 3, NA_TK, NA_TQ), lambda l, h: (l, h // 2, 0, 0, h % 2)),
        out_shape=jax.ShapeDtypeStruct((depth, NA_HEADS // 2, 3, NA_TK, 2 * NA_TQ), F32),
        scratch_shapes=[pltpu.VMEM((N_RPB_DR, GRID_W, GRID_W), F32)],
        compiler_params=pltpu.CompilerParams(
            dimension_semantics=("arbitrary", "arbitrary"), vmem_limit_bytes=VMEM_LIMIT),
        name="na_tables",
    )(rpb2)


def _na_kernel(tab_ref, q_ref, k_ref, vt_ref, o_ref, sa_ref, sb_ref, *, rows):
    s_refs = (sa_ref, sb_ref)
    lane = lax.broadcasted_iota(jnp.int32, (1, LANES), 1)
    vdim = lax.broadcasted_iota(jnp.int32, (LANES, 1), 0)
    nt = (((1,), (1,)), ((), ()))
    geo = _na_block_geometry(rows)
    units = []
    for rb, (ks, _) in enumerate(geo):
        if units and geo[units[-1][0]][0] == ks:
            units[-1].append(rb)
        else:
            units.append([rb])
    assert max(len(unit) for unit in units) <= NA_MAX_MERGE

    def kslice(unit):
        ks = geo[unit[0]][0]
        return slice(ks * GRID_W, ks * GRID_W + NA_TK)

    row0 = pl.multiple_of(jnp.minimum(pl.program_id(1), 0), 8)

    def scores(u):
        unit = units[u]
        width = len(unit) * 2 * NA_TQ
        parts = []
        for rb in unit:
            q = q_ref[0, rb * NA_TQ:(rb + 1) * NA_TQ, :]
            zero = jnp.zeros_like(q)
            parts += [jnp.where(lane < NA_HEAD_DIM, q, zero), jnp.where(lane < NA_HEAD_DIM, zero, q)]
        qst = jnp.concatenate(parts, axis=0)
        s = lax.dot_general(k_ref[0, kslice(unit), :], qst, nt, preferred_element_type=F32)
        s = s + jnp.concatenate([tab_ref[0, 0, geo[rb][1]] for rb in unit], axis=1)
        s_refs[u % 2][:, 0:width] = s
        return jnp.max(s, axis=0, keepdims=True)

    m = scores(0)
    for u, unit in enumerate(units):
        width = len(unit) * 2 * NA_TQ
        m_next = scores(u + 1) if u + 1 < len(units) else None
        e = jnp.exp2(s_refs[u % 2][pl.ds(row0, NA_TK), 0:width] - m)
        pv = jnp.dot(vt_ref[:, kslice(unit)], e.astype(BF16), preferred_element_type=F32)
        on = pv[:LANES] * (1.0 / pv[LANES:LANES + 1])
        for n, rb in enumerate(unit):
            h0 = on[:, 2 * n * NA_TQ:(2 * n + 1) * NA_TQ]
            h1 = on[:, (2 * n + 1) * NA_TQ:(2 * n + 2) * NA_TQ]
            ot = jnp.where(vdim < NA_HEAD_DIM, h0, h1)
            o_ref[0, rb * NA_TQ:(rb + 1) * NA_TQ, :] = ot.T.astype(BF16)
        m = m_next


def _neigh_attention(z3, vt, tables, layer):
    b, s, _ = z3.shape
    rows = s // GRID_W
    return pl.pallas_call(
        functools.partial(_na_kernel, rows=rows),
        grid=(NA_HEADS // 2, b),
        in_specs=[
            pl.BlockSpec((1, 1, 3, NA_TK, 2 * NA_TQ), lambda hp, i: (layer, hp, 0, 0, 0)),
            pl.BlockSpec((1, s, LANES), lambda hp, i: (i, 0, QN_BLK + hp)),
            pl.BlockSpec((1, s, LANES), lambda hp, i: (i, 0, KN_BLK + hp)),
            pl.BlockSpec((VT_ROWS, s), lambda hp, i: (VT_NA_GROUP0 + hp, i)),
        ],
        out_specs=pl.BlockSpec((1, s, LANES), lambda hp, i: (i, 0, hp)),
        out_shape=jax.ShapeDtypeStruct((b, s, NA_HEADS * NA_HEAD_DIM), BF16),
        scratch_shapes=[pltpu.VMEM((NA_TK, NA_MAX_MERGE * 2 * NA_TQ), F32)] * 2,
        compiler_params=pltpu.CompilerParams(
            dimension_semantics=("arbitrary", "arbitrary"), vmem_limit_bytes=VMEM_LIMIT),
        name="neigh_attn",
    )(tables, z3, z3, vt)


def _gated_out(x_ref, oda_ref, ona_ref, gate_ref, w_ref):
    gate = gate_ref[...].astype(F32)
    o = jnp.concatenate([oda_ref[...], ona_ref[...]], axis=-1).astype(F32) * (gate * jax.nn.sigmoid(gate))
    return x_ref[...] + jnp.dot(o, w_ref[...], preferred_element_type=F32)


def _outproj_final_kernel(x_ref, oda_ref, ona_ref, gate_ref, w_ref, fg_ref, xo_ref):
    xo_ref[...] = _rms(_gated_out(x_ref, oda_ref, ona_ref, gate_ref, w_ref), fg_ref[...])


def _layer_boundary_kernel(x_ref, oda_ref, ona_ref, gate_ref, wout_ref, g_ref, win_ref,
                           xo_ref, z_ref, vt_ref):
    xo = _gated_out(x_ref, oda_ref, ona_ref, gate_ref, wout_ref)
    xo_ref[...] = xo
    _project_in(xo, g_ref, win_ref, z_ref, vt_ref)


def _token_specs(layer):
    half = D_MODEL // 2
    return [
        pl.BlockSpec((TM_PROJ, D_MODEL), lambda i: (i, 0)),
        pl.BlockSpec((TM_PROJ, half), lambda i: (i, 0)),
        pl.BlockSpec((TM_PROJ, half), lambda i: (i, 0)),
        pl.BlockSpec((TM_PROJ, D_MODEL), lambda i: (i, GATE_BLK)),
        _resident((D_MODEL, D_MODEL), layer),
    ]


def _outproj_final(x2, oda2, ona2, z2, w_out, layer, final_g):
    m = x2.shape[0]
    return pl.pallas_call(
        _outproj_final_kernel,
        grid=(m // TM_PROJ,),
        in_specs=_token_specs(layer) + [pl.BlockSpec((1, D_MODEL), lambda i: (0, 0))],
        out_specs=pl.BlockSpec((TM_PROJ, D_MODEL), lambda i: (i, 0)),
        out_shape=jax.ShapeDtypeStruct((m, D_MODEL), F32),
        compiler_params=pltpu.CompilerParams(
            dimension_semantics=("arbitrary",), vmem_limit_bytes=VMEM_LIMIT),
        name="outproj_final",
    )(x2, oda2, ona2, z2, w_out, final_g)


def _layer_boundary(x2, oda2, ona2, z2, w_out, norm_g, w_in, layer):
    m = x2.shape[0]
    return pl.pallas_call(
        _layer_boundary_kernel,
        grid=(m // TM_PROJ,),
        in_specs=_token_specs(layer) + [_resident((1, D_MODEL), layer + 1),
                                        _resident((D_MODEL, IN_WIDTH), layer + 1)],
        out_specs=[pl.BlockSpec((TM_PROJ, D_MODEL), lambda i: (i, 0))] + _z_out_specs(),
        out_shape=[jax.ShapeDtypeStruct((m, D_MODEL), F32)] + _z_out_shapes(m),
        compiler_params=pltpu.CompilerParams(
            dimension_semantics=("arbitrary",), vmem_limit_bytes=VMEM_LIMIT_BOUNDARY),
        name="layer_boundary",
    )(x2, oda2, ona2, z2, w_out, norm_g, w_in)


def kernel(x, norm_g, w_in, w_out, lam_q1, lam_k1, lam_q2, lam_k2, subln_g, rpb, final_g):
    b, s, d = x.shape
    depth = norm_g.shape[0]
    assert d == D_MODEL and s % TQ_DA == 0 and s % (GRID_W * NA_QROWS) == 0
    rows = s // GRID_W
    assert rows >= NA_KROWS
    m = b * s
    slopes = np.asarray(2.0 ** (-8.0 * np.arange(1, DA_HEADS + 1) / DA_HEADS), dtype=np.float32)
    tables = _na_tables(rpb, rows)
    fg = final_g.reshape(1, d)
    norm_g3 = norm_g.reshape(depth, 1, d)
    x2 = x.reshape(m, d)
    z2, vt = _inproj(x2, norm_g3, w_in, 0)
    for l in range(depth):
        lam_init = 0.8 - 0.6 * math.exp(-0.3 * l)
        scalars = jnp.asarray(np.concatenate(
            [slopes, np.asarray([lam_init, 1.0 - lam_init, 0.0, 0.0], np.float32)]))
        lamv = jnp.stack([lam_q1[l], lam_k1[l], lam_q2[l], lam_k2[l]])
        z3 = z2.reshape(b, s, IN_WIDTH)
        oda = _diff_attention(z3, vt, scalars, lamv, subln_g[l].reshape(-1, 1)).reshape(m, -1)
        ona = _neigh_attention(z3, vt, tables, l).reshape(m, -1)
        if l + 1 < depth:
            x2, z2, vt = _layer_boundary(x2, oda, ona, z2, w_out, norm_g3, w_in, l)
        else:
            x2 = _outproj_final(x2, oda, ona, z2, w_out, l, fg)
    return x2.reshape(b, s, d)
```

```python
import functools
import math

import numpy as np
import jax
import jax.numpy as jnp
from jax import lax
from jax.experimental import pallas as pl
from jax.experimental.pallas import tpu as pltpu

F32 = jnp.float32
BF16 = jnp.bfloat16

D_MODEL = 1024
DEPTH = 4
GRID_W = 64
DA_HEADS = 4
DA_HEAD_DIM = 64
NA_HEADS = 8
NA_HEAD_DIM = 64
NA_ROWS_MAX = 8
NA_COLS = 16
RMS_EPS = 1e-6
IN_WIDTH = 4096
LANES = 128

QD_BLK, KD_BLK, VD_BLK = 0, 4, 8
QN_BLK, KN_BLK, VN_BLK = 12, 16, 20
GATE_BLK = 3

TM_PROJ = 512
TM_FINAL = 1024
TQ_DA = 256
TK_DA = TQ_DA
SUM_ROWS = 16
NA_QROWS = 4
NA_KROWS = 12
NA_MAX_MERGE = 2
NA_GROUP = 2
assert QN_BLK % NA_GROUP == 0 and KN_BLK % NA_GROUP == 0
NA_TQ = NA_QROWS * GRID_W
NA_TK = NA_KROWS * GRID_W
N_RPB_DR = 2 * NA_ROWS_MAX - 1
N_RPB_DC = 2 * NA_COLS - 1
VMEM_LIMIT = 48 * 1024 * 1024
VMEM_LIMIT_BOUNDARY = 56 * 1024 * 1024
LOG2E = math.log2(math.e)
Q_SCALE = DA_HEAD_DIM ** -0.5 * LOG2E
assert NA_HEAD_DIM == DA_HEAD_DIM
Z_CHUNK = 512
Q_CHUNKS = (QD_BLK * LANES // Z_CHUNK, QN_BLK * LANES // Z_CHUNK)
V_CHUNKS = (VD_BLK * LANES // Z_CHUNK, VN_BLK * LANES // Z_CHUNK)
VT_ROWS = LANES + SUM_ROWS
VT_GROUPS = len(V_CHUNKS) * Z_CHUNK // LANES
VT_NA_GROUP0 = Z_CHUNK // LANES


def _rms(x, g):
    return x * lax.rsqrt(jnp.mean(x * x, axis=-1, keepdims=True) + RMS_EPS) * g


def _project_in(x, g_ref, w_ref, z_ref, vt_ref):
    h = _rms(x, g_ref[...])
    ones = jnp.ones((SUM_ROWS, x.shape[0]), BF16)
    for j in range(IN_WIDTH // Z_CHUNK):
        cols = slice(j * Z_CHUNK, (j + 1) * Z_CHUNK)
        zc = jnp.dot(h, w_ref[:, cols], preferred_element_type=F32)
        if j in Q_CHUNKS:
            zc = zc * Q_SCALE
        z_ref[:, cols] = zc.astype(BF16)
        if j in V_CHUNKS:
            zt = zc.T.astype(BF16)
            for t in range(Z_CHUNK // LANES):
                grp = V_CHUNKS.index(j) * (Z_CHUNK // LANES) + t
                vt_ref[grp * VT_ROWS:grp * VT_ROWS + LANES, :] = zt[t * LANES:(t + 1) * LANES]
                vt_ref[grp * VT_ROWS + LANES:(grp + 1) * VT_ROWS, :] = ones


def _inproj_kernel(x_ref, g_ref, w_ref, z_ref, vt_ref):
    _project_in(x_ref[...], g_ref, w_ref, z_ref, vt_ref)


def _resident(shape, layer):
    return pl.BlockSpec((None,) + shape, lambda i: (layer,) + (0,) * len(shape),
                        pipeline_mode=pl.Buffered(1))


def _inproj(x2, norm_g, w_in, layer):
    m = x2.shape[0]
    return pl.pallas_call(
        _inproj_kernel,
        grid=(m // TM_PROJ,),
        in_specs=[
            pl.BlockSpec((TM_PROJ, D_MODEL), lambda i: (i, 0)),
            _resident((1, D_MODEL), layer),
            _resident((D_MODEL, IN_WIDTH), layer),
        ],
        out_specs=_z_out_specs(),
        out_shape=_z_out_shapes(m),
        compiler_params=pltpu.CompilerParams(
            dimension_semantics=("arbitrary",), vmem_limit_bytes=VMEM_LIMIT),
        name="inproj",
    )(x2, norm_g, w_in)


def _z_out_specs():
    return [pl.BlockSpec((TM_PROJ, IN_WIDTH), lambda i: (i, 0)),
            pl.BlockSpec((VT_GROUPS * VT_ROWS, TM_PROJ), lambda i: (0, i))]


def _z_out_shapes(m):
    return [jax.ShapeDtypeStruct((m, IN_WIDTH), BF16),
            jax.ShapeDtypeStruct((VT_GROUPS * VT_ROWS, m), BF16)]


def _da_kernel(sc_ref, lamv_ref, g_ref, q_ref, k_ref, vt_ref, o_ref,
               bias_ref, sa_ref, sb_ref, pa_ref, pb_ref, *, seq):
    nq = seq // TQ_DA
    nk = seq // TK_DA
    h = pl.program_id(0)
    lam_init = sc_ref[4]
    out_scale = sc_ref[5]

    @pl.when(pl.program_id(1) == 0)
    def _():
        slope = sc_ref[h] * LOG2E
        kr = lax.broadcasted_iota(jnp.int32, (TK_DA, TQ_DA), 0)
        qc = lax.broadcasted_iota(jnp.int32, (TK_DA, TQ_DA), 1)
        for t in range(nq + nk - 1):
            bias_ref[t] = slope * jnp.abs(kr - qc + (t - (nq - 1)) * TQ_DA).astype(F32)

    lv = lamv_ref[...]
    lam = (jnp.exp(jnp.sum(lv[0:1] * lv[1:2], axis=-1, keepdims=True))
           - jnp.exp(jnp.sum(lv[2:3] * lv[3:4], axis=-1, keepdims=True)) + lam_init)
    lane = lax.broadcasted_iota(jnp.int32, (1, LANES), 1)
    g = g_ref[...]
    row0 = pl.multiple_of(jnp.minimum(pl.program_id(1), 0), 8)
    nt = (((1,), (1,)), ((), ()))

    def rows_of(blk):
        if isinstance(blk, int):
            return slice(blk * TQ_DA, (blk + 1) * TQ_DA)
        return pl.ds(pl.multiple_of(blk * TQ_DA, TQ_DA), TQ_DA)

    def scores(blk, s_ref):
        q = q_ref[0, rows_of(blk), :]
        zero = jnp.zeros_like(q)
        qst = jnp.concatenate([jnp.where(lane < DA_HEAD_DIM, q, zero),
                               jnp.where(lane < DA_HEAD_DIM, zero, q)], axis=0)
        bias = jnp.concatenate([bias_ref[j - blk + nq - 1] for j in range(nk)], axis=0)
        s = lax.dot_general(k_ref[0], qst, nt, preferred_element_type=F32)
        s = s - jnp.concatenate([bias, bias], axis=1)
        s_ref[...] = s
        return jnp.max(s, axis=0, keepdims=True)

    def attend(s_ref, m, pv_ref):
        e = jnp.exp2(s_ref[pl.ds(row0, seq), :] - m)
        pv_ref[...] = jnp.dot(vt_ref[...], e.astype(BF16), preferred_element_type=F32)

    def finalize(blk, pv_ref):
        pv, l = pv_ref[0:LANES, :], pv_ref[LANES:LANES + 1, :]
        ot = pv[:, :TQ_DA] * (1.0 / l[:, :TQ_DA]) - pv[:, TQ_DA:] * (lam / l[:, TQ_DA:])
        ot = ot * lax.rsqrt(jnp.mean(ot * ot, axis=0, keepdims=True) + RMS_EPS) * g * out_scale
        o_ref[0, rows_of(blk), :] = ot.T.astype(BF16)

    def pair(p, m_even, first=False, last=False):
        if not first:
            finalize(2 * p - 1, pb_ref)
        m_odd = scores(2 * p + 1, sb_ref)
        attend(sa_ref, m_even, pa_ref)
        m_next = None if last else scores(2 * p + 2, sa_ref)
        finalize(2 * p, pa_ref)
        attend(sb_ref, m_odd, pb_ref)
        return m_next

    m_even = pair(0, scores(0, sa_ref), first=True)
    for p in range(1, nq // 2 - 1):
        m_even = pair(p, m_even)
    pair(nq // 2 - 1, m_even, last=True)
    finalize(nq - 1, pb_ref)


def _diff_attention(z3, vt, scalars, lamv, subln_g):
    b, s, _ = z3.shape
    smem = pl.BlockSpec(memory_space=pltpu.SMEM)
    nq, nk = s // TQ_DA, s // TK_DA
    return pl.pallas_call(
        functools.partial(_da_kernel, seq=s),
        grid=(DA_HEADS, b),
        in_specs=[
            smem,
            pl.BlockSpec((4, DA_HEAD_DIM), lambda h, i: (0, 0)),
            pl.BlockSpec((2 * DA_HEAD_DIM, 1), lambda h, i: (0, 0)),
            pl.BlockSpec((1, s, LANES), lambda h, i: (i, 0, QD_BLK + h)),
            pl.BlockSpec((1, s, LANES), lambda h, i: (i, 0, KD_BLK + h)),
            pl.BlockSpec((VT_ROWS, s), lambda h, i: (h, i)),
        ],
        out_specs=pl.BlockSpec((1, s, LANES), lambda h, i: (i, 0, h)),
        out_shape=jax.ShapeDtypeStruct((b, s, DA_HEADS * LANES), BF16),
        scratch_shapes=[
            pltpu.VMEM((nq + nk - 1, TK_DA, TQ_DA), F32),
            pltpu.VMEM((s, 2 * TQ_DA), F32),
            pltpu.VMEM((s, 2 * TQ_DA), F32),
            pltpu.VMEM((LANES + SUM_ROWS, 2 * TQ_DA), F32),
            pltpu.VMEM((LANES + SUM_ROWS, 2 * TQ_DA), F32),
        ],
        compiler_params=pltpu.CompilerParams(
            dimension_semantics=("arbitrary", "arbitrary"), vmem_limit_bytes=VMEM_LIMIT),
        name="diff_attn",
    )(scalars, lamv, subln_g, z3, z3, vt)


def _na_block_geometry(rows):
    kr = min(NA_ROWS_MAX, rows)
    nblk = rows // NA_QROWS
    geo = []
    for rb in range(nblk):
        ks = min(max(NA_QROWS * rb - kr // 2, 0), rows - NA_KROWS)
        var = 0 if rb == 0 else (2 if rb == nblk - 1 else 1)
        geo.append((ks, var))
    return geo


def _na_table_dr(rows):
    kr = min(NA_ROWS_MAX, rows)
    geo = _na_block_geometry(rows)
    rep = {0: 0, 1: 1, 2: len(geo) - 1}
    out = {}
    for var, rb in rep.items():
        ks = geo[rb][0]
        for ri in range(NA_QROWS):
            qr = NA_QROWS * rb + ri
            rs = min(max(qr - kr // 2, 0), rows - kr)
            for kj in range(NA_KROWS):
                krow = ks + kj
                out[(var, ri, kj)] = (krow - qr + NA_ROWS_MAX - 1) if rs <= krow < rs + kr else None
    return out


def _na_tables_kernel(rpb_ref, tab_ref, dict_ref, *, rows):
    row = pl.program_id(0) * NA_HEADS + pl.program_id(1)
    kc = lax.broadcasted_iota(jnp.int32, (GRID_W, GRID_W), 0)
    qc = lax.broadcasted_iota(jnp.int32, (GRID_W, GRID_W), 1)
    diff = kc - qc + (NA_COLS - 1)
    qcs = jnp.clip(qc - NA_COLS // 2, 0, GRID_W - NA_COLS)
    col_valid = (kc >= qcs) & (kc < qcs + NA_COLS)
    neg = jnp.full((GRID_W, GRID_W), -jnp.inf, F32)
    for dr in range(N_RPB_DR):
        t = jnp.zeros((GRID_W, GRID_W), F32)
        for dc in range(N_RPB_DC):
            t = jnp.where(diff == dc, rpb_ref[row, dr * N_RPB_DC + dc], t)
        dict_ref[dr] = jnp.where(col_valid, t * LOG2E, neg)
    for (var, ri, kj), dr in _na_table_dr(rows).items():
        tile = neg if dr is None else dict_ref[dr]
        tab_ref[0, 0, var, kj * GRID_W:(kj + 1) * GRID_W, ri * GRID_W:(ri + 1) * GRID_W] = tile


def _na_tables(rpb, rows):
    depth = rpb.shape[0]
    rpb2 = rpb.reshape(depth * NA_HEADS, N_RPB_DR * N_RPB_DC)
    return pl.pallas_call(
        functools.partial(_na_tables_kernel, rows=rows),
        grid=(depth, NA_HEADS),
        in_specs=[pl.BlockSpec(memory_space=pltpu.SMEM)],
        out_specs=pl.BlockSpec((1, 1, 3, NA_TK, NA_TQ), lambda l, h: (l, h // 2, 0, 0, h % 2)),
        out_shape=jax.ShapeDtypeStruct((depth, NA_HEADS // 2, 3, NA_TK, 2 * NA_TQ), F32),
        scratch_shapes=[pltpu.VMEM((N_RPB_DR, GRID_W, GRID_W), F32)],
        compiler_params=pltpu.CompilerParams(
            dimension_semantics=("arbitrary", "arbitrary"), vmem_limit_bytes=VMEM_LIMIT),
        name="na_tables",
    )(rpb2)


def _na_kernel(tab_ref, q_ref, k_ref, vt_ref, o_ref, sa_ref, sb_ref, *, rows):
    s_refs = (sa_ref, sb_ref)
    lane = lax.broadcasted_iota(jnp.int32, (1, LANES), 1)
    vdim = lax.broadcasted_iota(jnp.int32, (LANES, 1), 0)
    nt = (((1,), (1,)), ((), ()))
    geo = _na_block_geometry(rows)
    units = []
    for rb, (ks, _) in enumerate(geo):
        if units and geo[units[-1][0]][0] == ks:
            units[-1].append(rb)
        else:
            units.append([rb])
    assert max(len(unit) for unit in units) <= NA_MAX_MERGE

    def kslice(unit):
        ks = geo[unit[0]][0]
        return slice(ks * GRID_W, ks * GRID_W + NA_TK)

    row0 = pl.multiple_of(jnp.minimum(pl.program_id(1), 0), 8)

    def scores(u):
        unit = units[u]
        pw = len(unit) * 2 * NA_TQ
        blocks, tabs = [], []
        for g in range(NA_GROUP):
            parts = []
            for rb in unit:
                q = q_ref[0, rb * NA_TQ:(rb + 1) * NA_TQ, g * LANES:(g + 1) * LANES]
                zero = jnp.zeros_like(q)
                parts += [jnp.where(lane < NA_HEAD_DIM, q, zero), jnp.where(lane < NA_HEAD_DIM, zero, q)]
                tabs.append(tab_ref[0, g, geo[rb][1]])
            qg = jnp.concatenate(parts, axis=0)
            zg = jnp.zeros_like(qg)
            blocks.append(jnp.concatenate([qg if gg == g else zg for gg in range(NA_GROUP)], axis=1))
        qbd = jnp.concatenate(blocks, axis=0)
        s = lax.dot_general(k_ref[0, kslice(unit), :], qbd, nt, preferred_element_type=F32)
        s = s + jnp.concatenate(tabs, axis=1)
        s_refs[u % 2][:, 0:NA_GROUP * pw] = s
        return jnp.max(s, axis=0, keepdims=True)

    m = scores(0)
    for u, unit in enumerate(units):
        pw = len(unit) * 2 * NA_TQ
        m_next = scores(u + 1) if u + 1 < len(units) else None
        e = jnp.exp2(s_refs[u % 2][pl.ds(row0, NA_TK), 0:NA_GROUP * pw] - m).astype(BF16)
        for g in range(NA_GROUP):
            pv = jnp.dot(vt_ref[g * VT_ROWS:(g + 1) * VT_ROWS, kslice(unit)], e[:, g * pw:(g + 1) * pw],
                         preferred_element_type=F32)
            on = pv[:LANES] * (1.0 / pv[LANES:LANES + 1])
            for n, rb in enumerate(unit):
                h0 = on[:, 2 * n * NA_TQ:(2 * n + 1) * NA_TQ]
                h1 = on[:, (2 * n + 1) * NA_TQ:(2 * n + 2) * NA_TQ]
                ot = jnp.where(vdim < NA_HEAD_DIM, h0, h1)
                o_ref[0, rb * NA_TQ:(rb + 1) * NA_TQ, g * LANES:(g + 1) * LANES] = ot.T.astype(BF16)
        m = m_next


def _neigh_attention(z3, vt, tables, layer):
    b, s, _ = z3.shape
    rows = s // GRID_W
    gw = NA_GROUP * LANES
    return pl.pallas_call(
        functools.partial(_na_kernel, rows=rows),
        grid=(NA_HEADS // 2 // NA_GROUP, b),
        in_specs=[
            pl.BlockSpec((1, NA_GROUP, 3, NA_TK, 2 * NA_TQ), lambda hp, i: (layer, hp, 0, 0, 0),
                         pipeline_mode=pl.Buffered(1)),
            pl.BlockSpec((1, s, gw), lambda hp, i: (i, 0, QN_BLK // NA_GROUP + hp)),
            pl.BlockSpec((1, s, gw), lambda hp, i: (i, 0, KN_BLK // NA_GROUP + hp)),
            pl.BlockSpec((NA_GROUP * VT_ROWS, s), lambda hp, i: (VT_NA_GROUP0 // NA_GROUP + hp, i)),
        ],
        out_specs=pl.BlockSpec((1, s, gw), lambda hp, i: (i, 0, hp)),
        out_shape=jax.ShapeDtypeStruct((b, s, NA_HEADS * NA_HEAD_DIM), BF16),
        scratch_shapes=[pltpu.VMEM((NA_TK, NA_GROUP * NA_MAX_MERGE * 2 * NA_TQ), F32)] * 2,
        compiler_params=pltpu.CompilerParams(
            dimension_semantics=("arbitrary", "arbitrary"), vmem_limit_bytes=VMEM_LIMIT),
        name="neigh_attn",
    )(tables, z3, z3, vt)


def _gated_out(x_ref, oda_ref, ona_ref, gate_ref, w_ref):
    gate = gate_ref[...].astype(F32)
    o = jnp.concatenate([oda_ref[...], ona_ref[...]], axis=-1).astype(F32) * (gate * jax.nn.sigmoid(gate))
    return x_ref[...] + jnp.dot(o, w_ref[...], preferred_element_type=F32)


def _outproj_final_kernel(x_ref, oda_ref, ona_ref, gate_ref, w_ref, fg_ref, xo_ref):
    xo_ref[...] = _rms(_gated_out(x_ref, oda_ref, ona_ref, gate_ref, w_ref), fg_ref[...])


def _layer_boundary_kernel(x_ref, oda_ref, ona_ref, gate_ref, wout_ref, g_ref, win_ref,
                           xo_ref, z_ref, vt_ref):
    xo = _gated_out(x_ref, oda_ref, ona_ref, gate_ref, wout_ref)
    xo_ref[...] = xo
    _project_in(xo, g_ref, win_ref, z_ref, vt_ref)


def _token_specs(layer, tm=TM_PROJ):
    half = D_MODEL // 2
    return [
        pl.BlockSpec((tm, D_MODEL), lambda i: (i, 0)),
        pl.BlockSpec((tm, half), lambda i: (i, 0)),
        pl.BlockSpec((tm, half), lambda i: (i, 0)),
        pl.BlockSpec((tm, D_MODEL), lambda i: (i, GATE_BLK)),
        _resident((D_MODEL, D_MODEL), layer),
    ]


def _outproj_final(x2, oda2, ona2, z2, w_out, layer, final_g):
    m = x2.shape[0]
    return pl.pallas_call(
        _outproj_final_kernel,
        grid=(m // TM_FINAL,),
        in_specs=_token_specs(layer, TM_FINAL) + [pl.BlockSpec((1, D_MODEL), lambda i: (0, 0))],
        out_specs=pl.BlockSpec((TM_FINAL, D_MODEL), lambda i: (i, 0)),
        out_shape=jax.ShapeDtypeStruct((m, D_MODEL), F32),
        compiler_params=pltpu.CompilerParams(
            dimension_semantics=("arbitrary",), vmem_limit_bytes=VMEM_LIMIT),
        name="outproj_final",
    )(x2, oda2, ona2, z2, w_out, final_g)


def _layer_boundary(x2, oda2, ona2, z2, w_out, norm_g, w_in, layer):
    m = x2.shape[0]
    return pl.pallas_call(
        _layer_boundary_kernel,
        grid=(m // TM_PROJ,),
        in_specs=_token_specs(layer) + [_resident((1, D_MODEL), layer + 1),
                                        _resident((D_MODEL, IN_WIDTH), layer + 1)],
        out_specs=[pl.BlockSpec((TM_PROJ, D_MODEL), lambda i: (i, 0))] + _z_out_specs(),
        out_shape=[jax.ShapeDtypeStruct((m, D_MODEL), F32)] + _z_out_shapes(m),
        compiler_params=pltpu.CompilerParams(
            dimension_semantics=("arbitrary",), vmem_limit_bytes=VMEM_LIMIT_BOUNDARY),
        name="layer_boundary",
    )(x2, oda2, ona2, z2, w_out, norm_g, w_in)


def kernel(x, norm_g, w_in, w_out, lam_q1, lam_k1, lam_q2, lam_k2, subln_g, rpb, final_g):
    b, s, d = x.shape
    depth = norm_g.shape[0]
    assert d == D_MODEL and s % TQ_DA == 0 and s % (GRID_W * NA_QROWS) == 0
    rows = s // GRID_W
    assert rows >= NA_KROWS
    m = b * s
    slopes = np.asarray(2.0 ** (-8.0 * np.arange(1, DA_HEADS + 1) / DA_HEADS), dtype=np.float32)
    tables = _na_tables(rpb, rows)
    fg = final_g.reshape(1, d)
    norm_g3 = norm_g.reshape(depth, 1, d)
    x2 = x.reshape(m, d)
    z2, vt = _inproj(x2, norm_g3, w_in, 0)
    for l in range(depth):
        lam_init = 0.8 - 0.6 * math.exp(-0.3 * l)
        scalars = jnp.asarray(np.concatenate(
            [slopes, np.asarray([lam_init, 1.0 - lam_init, 0.0, 0.0], np.float32)]))
        lamv = jnp.stack([lam_q1[l], lam_k1[l], lam_q2[l], lam_k2[l]])
        z3 = z2.reshape(b, s, IN_WIDTH)
        oda = _diff_attention(z3, vt, scalars, lamv, subln_g[l].reshape(-1, 1)).reshape(m, -1)
        ona = _neigh_attention(z3, vt, tables, l).reshape(m, -1)
        if l + 1 < depth:
            x2, z2, vt = _layer_boundary(x2, oda, ona, z2, w_out, norm_g3, w_in, l)
        else:
            x2 = _outproj_final(x2, oda, ona, z2, w_out, l, fg)
    return x2.reshape(b, s, d)
```

```python
import functools
import math

import numpy as np
import jax
import jax.numpy as jnp
from jax import lax
from jax.experimental import pallas as pl
from jax.experimental.pallas import tpu as pltpu

F32 = jnp.float32
BF16 = jnp.bfloat16

D_MODEL = 1024
DEPTH = 4
GRID_W = 64
DA_HEADS = 4
DA_HEAD_DIM = 64
NA_HEADS = 8
NA_HEAD_DIM = 64
NA_ROWS_MAX = 8
NA_COLS = 16
RMS_EPS = 1e-6
IN_WIDTH = 4096
LANES = 128

QD_BLK, KD_BLK, VD_BLK = 0, 4, 8
QN_BLK, KN_BLK, VN_BLK = 12, 16, 20
GATE_BLK = 3

TM_PROJ = 512
TM_FINAL = 1024
TQ_DA = 256
TK_DA = TQ_DA
SUM_ROWS = 16
NA_QROWS = 4
NA_KROWS = 12
NA_MAX_MERGE = 2
NA_GROUP = 2
assert QN_BLK % NA_GROUP == 0 and KN_BLK % NA_GROUP == 0
NA_TQ = NA_QROWS * GRID_W
NA_TK = NA_KROWS * GRID_W
N_RPB_DR = 2 * NA_ROWS_MAX - 1
N_RPB_DC = 2 * NA_COLS - 1
VMEM_LIMIT = 48 * 1024 * 1024
VMEM_LIMIT_BOUNDARY = 56 * 1024 * 1024
VMEM_LIMIT_NA = 56 * 1024 * 1024
LOG2E = math.log2(math.e)
Q_SCALE = DA_HEAD_DIM ** -0.5 * LOG2E
assert NA_HEAD_DIM == DA_HEAD_DIM
Z_CHUNK = 512
Q_CHUNKS = (QD_BLK * LANES // Z_CHUNK, QN_BLK * LANES // Z_CHUNK)
V_CHUNKS = (VD_BLK * LANES // Z_CHUNK, VN_BLK * LANES // Z_CHUNK)
VT_ROWS = LANES + SUM_ROWS
VT_GROUPS = len(V_CHUNKS) * Z_CHUNK // LANES
VT_NA_GROUP0 = Z_CHUNK // LANES


def _rms(x, g):
    return x * lax.rsqrt(jnp.mean(x * x, axis=-1, keepdims=True) + RMS_EPS) * g


def _project_in(x, g_ref, w_ref, z_ref, vt_ref):
    h = _rms(x, g_ref[...])
    ones = jnp.ones((SUM_ROWS, x.shape[0]), BF16)
    for j in range(IN_WIDTH // Z_CHUNK):
        cols = slice(j * Z_CHUNK, (j + 1) * Z_CHUNK)
        zc = jnp.dot(h, w_ref[:, cols], preferred_element_type=F32)
        if j in Q_CHUNKS:
            zc = zc * Q_SCALE
        z_ref[:, cols] = zc.astype(BF16)
        if j in V_CHUNKS:
            zt = zc.T.astype(BF16)
            for t in range(Z_CHUNK // LANES):
                grp = V_CHUNKS.index(j) * (Z_CHUNK // LANES) + t
                vt_ref[grp * VT_ROWS:grp * VT_ROWS + LANES, :] = zt[t * LANES:(t + 1) * LANES]
                vt_ref[grp * VT_ROWS + LANES:(grp + 1) * VT_ROWS, :] = ones


def _inproj_kernel(x_ref, g_ref, w_ref, z_ref, vt_ref):
    _project_in(x_ref[...], g_ref, w_ref, z_ref, vt_ref)


def _resident(shape, layer):
    return pl.BlockSpec((None,) + shape, lambda i: (layer,) + (0,) * len(shape),
                        pipeline_mode=pl.Buffered(1))


def _inproj(x2, norm_g, w_in, layer):
    m = x2.shape[0]
    return pl.pallas_call(
        _inproj_kernel,
        grid=(m // TM_PROJ,),
        in_specs=[
            pl.BlockSpec((TM_PROJ, D_MODEL), lambda i: (i, 0)),
            _resident((1, D_MODEL), layer),
            _resident((D_MODEL, IN_WIDTH), layer),
        ],
        out_specs=_z_out_specs(),
        out_shape=_z_out_shapes(m),
        compiler_params=pltpu.CompilerParams(
            dimension_semantics=("arbitrary",), vmem_limit_bytes=VMEM_LIMIT),
        name="inproj",
    )(x2, norm_g, w_in)


def _z_out_specs():
    return [pl.BlockSpec((TM_PROJ, IN_WIDTH), lambda i: (i, 0)),
            pl.BlockSpec((VT_GROUPS * VT_ROWS, TM_PROJ), lambda i: (0, i))]


def _z_out_shapes(m):
    return [jax.ShapeDtypeStruct((m, IN_WIDTH), BF16),
            jax.ShapeDtypeStruct((VT_GROUPS * VT_ROWS, m), BF16)]


def _da_kernel(sc_ref, lamv_ref, g_ref, q_ref, k_ref, vt_ref, o_ref,
               bias_ref, sa_ref, sb_ref, pa_ref, pb_ref, *, seq):
    nq = seq // TQ_DA
    nk = seq // TK_DA
    h = pl.program_id(0)
    lam_init = sc_ref[4]
    out_scale = sc_ref[5]

    @pl.when(pl.program_id(1) == 0)
    def _():
        slope = sc_ref[h] * LOG2E
        kr = lax.broadcasted_iota(jnp.int32, (TK_DA, TQ_DA), 0)
        qc = lax.broadcasted_iota(jnp.int32, (TK_DA, TQ_DA), 1)
        for t in range(nq + nk - 1):
            bias_ref[t] = slope * jnp.abs(kr - qc + (t - (nq - 1)) * TQ_DA).astype(F32)

    lv = lamv_ref[...]
    lam = (jnp.exp(jnp.sum(lv[0:1] * lv[1:2], axis=-1, keepdims=True))
           - jnp.exp(jnp.sum(lv[2:3] * lv[3:4], axis=-1, keepdims=True)) + lam_init)
    lane = lax.broadcasted_iota(jnp.int32, (1, LANES), 1)
    g = g_ref[...]
    row0 = pl.multiple_of(jnp.minimum(pl.program_id(1), 0), 8)
    nt = (((1,), (1,)), ((), ()))

    def rows_of(blk):
        if isinstance(blk, int):
            return slice(blk * TQ_DA, (blk + 1) * TQ_DA)
        return pl.ds(pl.multiple_of(blk * TQ_DA, TQ_DA), TQ_DA)

    def scores(blk, s_ref):
        q = q_ref[0, rows_of(blk), :]
        zero = jnp.zeros_like(q)
        qst = jnp.concatenate([jnp.where(lane < DA_HEAD_DIM, q, zero),
                               jnp.where(lane < DA_HEAD_DIM, zero, q)], axis=0)
        bias = jnp.concatenate([bias_ref[j - blk + nq - 1] for j in range(nk)], axis=0)
        s = lax.dot_general(k_ref[0], qst, nt, preferred_element_type=F32)
        s = s - jnp.concatenate([bias, bias], axis=1)
        s_ref[...] = s
        return jnp.max(s, axis=0, keepdims=True)

    def attend(s_ref, m, pv_ref):
        e = jnp.exp2(s_ref[pl.ds(row0, seq), :] - m)
        pv_ref[...] = jnp.dot(vt_ref[...], e.astype(BF16), preferred_element_type=F32)

    def finalize(blk, pv_ref):
        pv, l = pv_ref[0:LANES, :], pv_ref[LANES:LANES + 1, :]
        ot = pv[:, :TQ_DA] * (1.0 / l[:, :TQ_DA]) - pv[:, TQ_DA:] * (lam / l[:, TQ_DA:])
        ot = ot * lax.rsqrt(jnp.mean(ot * ot, axis=0, keepdims=True) + RMS_EPS) * g * out_scale
        o_ref[0, rows_of(blk), :] = ot.T.astype(BF16)

    def pair(p, m_even, first=False, last=False):
        if not first:
            finalize(2 * p - 1, pb_ref)
        m_odd = scores(2 * p + 1, sb_ref)
        attend(sa_ref, m_even, pa_ref)
        m_next = None if last else scores(2 * p + 2, sa_ref)
        finalize(2 * p, pa_ref)
        attend(sb_ref, m_odd, pb_ref)
        return m_next

    m_even = pair(0, scores(0, sa_ref), first=True)
    for p in range(1, nq // 2 - 1):
        m_even = pair(p, m_even)
    pair(nq // 2 - 1, m_even, last=True)
    finalize(nq - 1, pb_ref)


def _diff_attention(z3, vt, scalars, lamv, subln_g):
    b, s, _ = z3.shape
    smem = pl.BlockSpec(memory_space=pltpu.SMEM)
    nq, nk = s // TQ_DA, s // TK_DA
    return pl.pallas_call(
        functools.partial(_da_kernel, seq=s),
        grid=(DA_HEADS, b),
        in_specs=[
            smem,
            pl.BlockSpec((4, DA_HEAD_DIM), lambda h, i: (0, 0)),
            pl.BlockSpec((2 * DA_HEAD_DIM, 1), lambda h, i: (0, 0)),
            pl.BlockSpec((1, s, LANES), lambda h, i: (i, 0, QD_BLK + h)),
            pl.BlockSpec((1, s, LANES), lambda h, i: (i, 0, KD_BLK + h)),
            pl.BlockSpec((VT_ROWS, s), lambda h, i: (h, i)),
        ],
        out_specs=pl.BlockSpec((1, s, LANES), lambda h, i: (i, 0, h)),
        out_shape=jax.ShapeDtypeStruct((b, s, DA_HEADS * LANES), BF16),
        scratch_shapes=[
            pltpu.VMEM((nq + nk - 1, TK_DA, TQ_DA), F32),
            pltpu.VMEM((s, 2 * TQ_DA), F32),
            pltpu.VMEM((s, 2 * TQ_DA), F32),
            pltpu.VMEM((LANES + SUM_ROWS, 2 * TQ_DA), F32),
            pltpu.VMEM((LANES + SUM_ROWS, 2 * TQ_DA), F32),
        ],
        compiler_params=pltpu.CompilerParams(
            dimension_semantics=("arbitrary", "arbitrary"), vmem_limit_bytes=VMEM_LIMIT),
        name="diff_attn",
    )(scalars, lamv, subln_g, z3, z3, vt)


def _na_block_geometry(rows):
    kr = min(NA_ROWS_MAX, rows)
    nblk = rows // NA_QROWS
    geo = []
    for rb in range(nblk):
        ks = min(max(NA_QROWS * rb - kr // 2, 0), rows - NA_KROWS)
        var = 0 if rb == 0 else (2 if rb == nblk - 1 else 1)
        geo.append((ks, var))
    return geo


def _na_table_dr(rows):
    kr = min(NA_ROWS_MAX, rows)
    geo = _na_block_geometry(rows)
    rep = {0: 0, 1: 1, 2: len(geo) - 1}
    out = {}
    for var, rb in rep.items():
        ks = geo[rb][0]
        for ri in range(NA_QROWS):
            qr = NA_QROWS * rb + ri
            rs = min(max(qr - kr // 2, 0), rows - kr)
            for kj in range(NA_KROWS):
                krow = ks + kj
                out[(var, ri, kj)] = (krow - qr + NA_ROWS_MAX - 1) if rs <= krow < rs + kr else None
    return out


def _na_tables_kernel(rpb_ref, tab_ref, dict_ref, *, rows):
    row = pl.program_id(0) * NA_HEADS + pl.program_id(1)
    kc = lax.broadcasted_iota(jnp.int32, (GRID_W, GRID_W), 0)
    qc = lax.broadcasted_iota(jnp.int32, (GRID_W, GRID_W), 1)
    diff = kc - qc + (NA_COLS - 1)
    qcs = jnp.clip(qc - NA_COLS // 2, 0, GRID_W - NA_COLS)
    col_valid = (kc >= qcs) & (kc < qcs + NA_COLS)
    neg = jnp.full((GRID_W, GRID_W), -jnp.inf, F32)
    for dr in range(N_RPB_DR):
        t = jnp.zeros((GRID_W, GRID_W), F32)
        for dc in range(N_RPB_DC):
            t = jnp.where(diff == dc, rpb_ref[row, dr * N_RPB_DC + dc], t)
        dict_ref[dr] = jnp.where(col_valid, t * LOG2E, neg)
    for (var, ri, kj), dr in _na_table_dr(rows).items():
        tile = neg if dr is None else dict_ref[dr]
        tab_ref[0, 0, var, kj * GRID_W:(kj + 1) * GRID_W, ri * GRID_W:(ri + 1) * GRID_W] = tile


def _na_tables(rpb, rows):
    depth = rpb.shape[0]
    rpb2 = rpb.reshape(depth * NA_HEADS, N_RPB_DR * N_RPB_DC)
    return pl.pallas_call(
        functools.partial(_na_tables_kernel, rows=rows),
        grid=(depth, NA_HEADS),
        in_specs=[pl.BlockSpec(memory_space=pltpu.SMEM)],
        out_specs=pl.BlockSpec((1, 1, 3, NA_TK, NA_TQ), lambda l, h: (l, h // 2, 0, 0, h % 2)),
        out_shape=jax.ShapeDtypeStruct((depth, NA_HEADS // 2, 3, NA_TK, 2 * NA_TQ), F32),
        scratch_shapes=[pltpu.VMEM((N_RPB_DR, GRID_W, GRID_W), F32)],
        compiler_params=pltpu.CompilerParams(
            dimension_semantics=("arbitrary", "arbitrary"), vmem_limit_bytes=VMEM_LIMIT),
        name="na_tables",
    )(rpb2)


def _na_kernel(tab_ref, q_ref, k_ref, vt_ref, o_ref, sa_ref, sb_ref, *, rows):
    s_refs = (sa_ref, sb_ref)
    lane = lax.broadcasted_iota(jnp.int32, (1, LANES), 1)
    vdim = lax.broadcasted_iota(jnp.int32, (LANES, 1), 0)
    nt = (((1,), (1,)), ((), ()))
    geo = _na_block_geometry(rows)
    units = []
    for rb, (ks, _) in enumerate(geo):
        if units and geo[units[-1][0]][0] == ks:
            units[-1].append(rb)
        else:
            units.append([rb])
    assert max(len(unit) for unit in units) <= NA_MAX_MERGE

    def kslice(unit):
        ks = geo[unit[0]][0]
        return slice(ks * GRID_W, ks * GRID_W + NA_TK)

    row0 = pl.multiple_of(jnp.minimum(pl.program_id(1), 0), 8)

    def scores(u):
        unit = units[u]
        pw = len(unit) * 2 * NA_TQ
        blocks, tabs = [], []
        for g in range(NA_GROUP):
            parts = []
            for rb in unit:
                q = q_ref[0, rb * NA_TQ:(rb + 1) * NA_TQ, g * LANES:(g + 1) * LANES]
                zero = jnp.zeros_like(q)
                parts += [jnp.where(lane < NA_HEAD_DIM, q, zero), jnp.where(lane < NA_HEAD_DIM, zero, q)]
                tabs.append(tab_ref[0, g, geo[rb][1]])
            qg = jnp.concatenate(parts, axis=0)
            zg = jnp.zeros_like(qg)
            blocks.append(jnp.concatenate([qg if gg == g else zg for gg in range(NA_GROUP)], axis=1))
        qbd = jnp.concatenate(blocks, axis=0)
        s = lax.dot_general(k_ref[0, kslice(unit), :], qbd, nt, preferred_element_type=F32)
        s = s + jnp.concatenate(tabs, axis=1)
        s_refs[u % 2][:, 0:NA_GROUP * pw] = s
        return jnp.max(s, axis=0, keepdims=True)

    m = scores(0)
    for u, unit in enumerate(units):
        pw = len(unit) * 2 * NA_TQ
        m_next = scores(u + 1) if u + 1 < len(units) else None
        e = jnp.exp2(s_refs[u % 2][pl.ds(row0, NA_TK), 0:NA_GROUP * pw] - m).astype(BF16)
        for g in range(NA_GROUP):
            pv = jnp.dot(vt_ref[g * VT_ROWS:(g + 1) * VT_ROWS, kslice(unit)], e[:, g * pw:(g + 1) * pw],
                         preferred_element_type=F32)
            on = pv[:LANES] * (1.0 / pv[LANES:LANES + 1])
            for n, rb in enumerate(unit):
                h0 = on[:, 2 * n * NA_TQ:(2 * n + 1) * NA_TQ]
                h1 = on[:, (2 * n + 1) * NA_TQ:(2 * n + 2) * NA_TQ]
                ot = jnp.where(vdim < NA_HEAD_DIM, h0, h1)
                o_ref[0, rb * NA_TQ:(rb + 1) * NA_TQ, g * LANES:(g + 1) * LANES] = ot.T.astype(BF16)
        m = m_next


def _neigh_attention(z3, vt, tables, layer):
    b, s, _ = z3.shape
    rows = s // GRID_W
    gw = NA_GROUP * LANES
    return pl.pallas_call(
        functools.partial(_na_kernel, rows=rows),
        grid=(NA_HEADS // 2 // NA_GROUP, b),
        in_specs=[
            pl.BlockSpec((1, NA_GROUP, 3, NA_TK, 2 * NA_TQ), lambda hp, i: (layer, hp, 0, 0, 0)),
            pl.BlockSpec((1, s, gw), lambda hp, i: (i, 0, QN_BLK // NA_GROUP + hp)),
            pl.BlockSpec((1, s, gw), lambda hp, i: (i, 0, KN_BLK // NA_GROUP + hp)),
            pl.BlockSpec((NA_GROUP * VT_ROWS, s), lambda hp, i: (VT_NA_GROUP0 // NA_GROUP + hp, i)),
        ],
        out_specs=pl.BlockSpec((1, s, gw), lambda hp, i: (i, 0, hp)),
        out_shape=jax.ShapeDtypeStruct((b, s, NA_HEADS * NA_HEAD_DIM), BF16),
        scratch_shapes=[pltpu.VMEM((NA_TK, NA_GROUP * NA_MAX_MERGE * 2 * NA_TQ), F32)] * 2,
        compiler_params=pltpu.CompilerParams(
            dimension_semantics=("arbitrary", "arbitrary"), vmem_limit_bytes=VMEM_LIMIT_NA),
        name="neigh_attn",
    )(tables, z3, z3, vt)


def _gated_out(x_ref, oda_ref, ona_ref, gate_ref, w_ref):
    gate = gate_ref[...].astype(F32)
    o = jnp.concatenate([oda_ref[...], ona_ref[...]], axis=-1).astype(F32) * (gate * jax.nn.sigmoid(gate))
    return x_ref[...] + jnp.dot(o, w_ref[...], preferred_element_type=F32)


def _outproj_final_kernel(x_ref, oda_ref, ona_ref, gate_ref, w_ref, fg_ref, xo_ref):
    xo_ref[...] = _rms(_gated_out(x_ref, oda_ref, ona_ref, gate_ref, w_ref), fg_ref[...])


def _layer_boundary_kernel(x_ref, oda_ref, ona_ref, gate_ref, wout_ref, g_ref, win_ref,
                           xo_ref, z_ref, vt_ref):
    xo = _gated_out(x_ref, oda_ref, ona_ref, gate_ref, wout_ref)
    xo_ref[...] = xo
    _project_in(xo, g_ref, win_ref, z_ref, vt_ref)


def _token_specs(layer, tm=TM_PROJ):
    half = D_MODEL // 2
    return [
        pl.BlockSpec((tm, D_MODEL), lambda i: (i, 0)),
        pl.BlockSpec((tm, half), lambda i: (i, 0)),
        pl.BlockSpec((tm, half), lambda i: (i, 0)),
        pl.BlockSpec((tm, D_MODEL), lambda i: (i, GATE_BLK)),
        _resident((D_MODEL, D_MODEL), layer),
    ]


def _outproj_final(x2, oda2, ona2, z2, w_out, layer, final_g):
    m = x2.shape[0]
    return pl.pallas_call(
        _outproj_final_kernel,
        grid=(m // TM_FINAL,),
        in_specs=_token_specs(layer, TM_FINAL) + [pl.BlockSpec((1, D_MODEL), lambda i: (0, 0))],
        out_specs=pl.BlockSpec((TM_FINAL, D_MODEL), lambda i: (i, 0)),
        out_shape=jax.ShapeDtypeStruct((m, D_MODEL), F32),
        compiler_params=pltpu.CompilerParams(
            dimension_semantics=("arbitrary",), vmem_limit_bytes=VMEM_LIMIT),
        name="outproj_final",
    )(x2, oda2, ona2, z2, w_out, final_g)


def _layer_boundary(x2, oda2, ona2, z2, w_out, norm_g, w_in, layer):
    m = x2.shape[0]
    return pl.pallas_call(
        _layer_boundary_kernel,
        grid=(m // TM_PROJ,),
        in_specs=_token_specs(layer) + [_resident((1, D_MODEL), layer + 1),
                                        _resident((D_MODEL, IN_WIDTH), layer + 1)],
        out_specs=[pl.BlockSpec((TM_PROJ, D_MODEL), lambda i: (i, 0))] + _z_out_specs(),
        out_shape=[jax.ShapeDtypeStruct((m, D_MODEL), F32)] + _z_out_shapes(m),
        compiler_params=pltpu.CompilerParams(
            dimension_semantics=("arbitrary",), vmem_limit_bytes=VMEM_LIMIT_BOUNDARY),
        name="layer_boundary",
    )(x2, oda2, ona2, z2, w_out, norm_g, w_in)


def kernel(x, norm_g, w_in, w_out, lam_q1, lam_k1, lam_q2, lam_k2, subln_g, rpb, final_g):
    b, s, d = x.shape
    depth = norm_g.shape[0]
    assert d == D_MODEL and s % TQ_DA == 0 and s % (GRID_W * NA_QROWS) == 0
    rows = s // GRID_W
    assert rows >= NA_KROWS
    m = b * s
    slopes = np.asarray(2.0 ** (-8.0 * np.arange(1, DA_HEADS + 1) / DA_HEADS), dtype=np.float32)
    tables = _na_tables(rpb, rows)
    fg = final_g.reshape(1, d)
    norm_g3 = norm_g.reshape(depth, 1, d)
    x2 = x.reshape(m, d)
    z2, vt = _inproj(x2, norm_g3, w_in, 0)
    for l in range(depth):
        lam_init = 0.8 - 0.6 * math.exp(-0.3 * l)
        scalars = jnp.asarray(np.concatenate(
            [slopes, np.asarray([lam_init, 1.0 - lam_init, 0.0, 0.0], np.float32)]))
        lamv = jnp.stack([lam_q1[l], lam_k1[l], lam_q2[l], lam_k2[l]])
        z3 = z2.reshape(b, s, IN_WIDTH)
        oda = _diff_attention(z3, vt, scalars, lamv, subln_g[l].reshape(-1, 1)).reshape(m, -1)
        ona = _neigh_attention(z3, vt, tables, l).reshape(m, -1)
        if l + 1 < depth:
            x2, z2, vt = _layer_boundary(x2, oda, ona, z2, w_out, norm_g3, w_in, l)
        else:
            x2 = _outproj_final(x2, oda, ona, z2, w_out, l, fg)
    return x2.reshape(b, s, d)
```

```python
import functools
import math

import numpy as np
import jax
import jax.numpy as jnp
from jax import lax
from jax.experimental import pallas as pl
from jax.experimental.pallas import tpu as pltpu

F32 = jnp.float32
BF16 = jnp.bfloat16

D_MODEL = 1024
DEPTH = 4
GRID_W = 64
DA_HEADS = 4
DA_HEAD_DIM = 64
NA_HEADS = 8
NA_HEAD_DIM = 64
NA_ROWS_MAX = 8
NA_COLS = 16
RMS_EPS = 1e-6
IN_WIDTH = 4096
LANES = 128
SUBLANES = 8

QD_BLK, KD_BLK, VD_BLK = 0, 4, 8
QN_BLK, KN_BLK, VN_BLK = 12, 16, 20
GATE_BLK = 3

TM_PROJ = 512
TM_FINAL = 1024
TQ_DA = 256
TK_DA = TQ_DA
SUM_ROWS = 16
NA_QROWS = 4
NA_KROWS = 12
NA_MAX_MERGE = 2
NA_GROUP = 2
assert QN_BLK % NA_GROUP == 0 and KN_BLK % NA_GROUP == 0
NA_TQ = NA_QROWS * GRID_W
NA_TK = NA_KROWS * GRID_W
N_RPB_DR = 2 * NA_ROWS_MAX - 1
N_RPB_DC = 2 * NA_COLS - 1
VMEM_LIMIT = 48 * 1024 * 1024
VMEM_LIMIT_BOUNDARY = 56 * 1024 * 1024
VMEM_LIMIT_NA = 56 * 1024 * 1024
LOG2E = math.log2(math.e)
Q_SCALE = DA_HEAD_DIM ** -0.5 * LOG2E
assert NA_HEAD_DIM == DA_HEAD_DIM
Z_CHUNK = 512
Q_CHUNKS = (QD_BLK * LANES // Z_CHUNK, QN_BLK * LANES // Z_CHUNK)
V_CHUNKS = (VD_BLK * LANES // Z_CHUNK, VN_BLK * LANES // Z_CHUNK)
VT_ROWS = LANES + SUM_ROWS
VT_GROUPS = len(V_CHUNKS) * Z_CHUNK // LANES
VT_NA_GROUP0 = Z_CHUNK // LANES


def _rms(x, g):
    return x * lax.rsqrt(jnp.mean(x * x, axis=-1, keepdims=True) + RMS_EPS) * g


def _project_in(x, g_ref, w_ref, z_ref, vt_ref):
    h = _rms(x, g_ref[...])
    ones = jnp.ones((SUM_ROWS, x.shape[0]), BF16)
    for j in range(IN_WIDTH // Z_CHUNK):
        cols = slice(j * Z_CHUNK, (j + 1) * Z_CHUNK)
        zc = jnp.dot(h, w_ref[:, cols], preferred_element_type=F32)
        if j in Q_CHUNKS:
            zc = zc * Q_SCALE
        z_ref[:, cols] = zc.astype(BF16)
        if j in V_CHUNKS:
            zt = zc.T.astype(BF16)
            for t in range(Z_CHUNK // LANES):
                grp = V_CHUNKS.index(j) * (Z_CHUNK // LANES) + t
                vt_ref[grp * VT_ROWS:grp * VT_ROWS + LANES, :] = zt[t * LANES:(t + 1) * LANES]
                vt_ref[grp * VT_ROWS + LANES:(grp + 1) * VT_ROWS, :] = ones


def _inproj_kernel(x_ref, g_ref, w_ref, rpb_ref, z_ref, vt_ref, tab_ref, dict_ref, *, rows):
    _project_in(x_ref[...], g_ref, w_ref, z_ref, vt_ref)
    _build_na_table(rpb_ref, tab_ref, dict_ref, rows)


def _resident(shape, layer):
    return pl.BlockSpec((None,) + shape, lambda i: (layer,) + (0,) * len(shape),
                        pipeline_mode=pl.Buffered(1))


def _inproj(x2, norm_g, w_in, rpb, rows):
    m = x2.shape[0]
    depth = rpb.shape[0]
    steps = m // TM_PROJ
    assert steps == depth * NA_HEADS, "one (layer, head) bias table is built per projection step"
    rpb_rev = jnp.pad(rpb[..., ::-1], ((0, 0), (0, 0), (0, 0), (0, LANES - N_RPB_DC)))
    rpb_rev = rpb_rev.reshape(depth * NA_HEADS, N_RPB_DR, LANES)
    tab_spec = pl.BlockSpec((1, 1, 3, NA_TK, NA_TQ),
                            lambda i: (i // NA_HEADS, (i % NA_HEADS) // 2, 0, 0, i % 2))
    tab_shape = jax.ShapeDtypeStruct((depth, NA_HEADS // 2, 3, NA_TK, 2 * NA_TQ), F32)
    return pl.pallas_call(
        functools.partial(_inproj_kernel, rows=rows),
        grid=(steps,),
        in_specs=[
            pl.BlockSpec((TM_PROJ, D_MODEL), lambda i: (i, 0)),
            _resident((1, D_MODEL), 0),
            _resident((D_MODEL, IN_WIDTH), 0),
            pl.BlockSpec((1, N_RPB_DR, LANES), lambda i: (i, 0, 0)),
        ],
        out_specs=_z_out_specs() + [tab_spec],
        out_shape=_z_out_shapes(m) + [tab_shape],
        scratch_shapes=[pltpu.VMEM((N_RPB_DR, GRID_W, GRID_W), F32)],
        compiler_params=pltpu.CompilerParams(
            dimension_semantics=("arbitrary",), vmem_limit_bytes=VMEM_LIMIT),
        name="inproj",
    )(x2, norm_g, w_in, rpb_rev)


def _z_out_specs():
    return [pl.BlockSpec((TM_PROJ, IN_WIDTH), lambda i: (i, 0)),
            pl.BlockSpec((VT_GROUPS * VT_ROWS, TM_PROJ), lambda i: (0, i))]


def _z_out_shapes(m):
    return [jax.ShapeDtypeStruct((m, IN_WIDTH), BF16),
            jax.ShapeDtypeStruct((VT_GROUPS * VT_ROWS, m), BF16)]


def _da_kernel(sc_ref, lamv_ref, g_ref, q_ref, k_ref, vt_ref, o_ref,
               bias_ref, sa_ref, sb_ref, pa_ref, pb_ref, *, seq):
    nq = seq // TQ_DA
    nk = seq // TK_DA
    h = pl.program_id(0)
    lam_init = sc_ref[4]
    out_scale = sc_ref[5]

    @pl.when(pl.program_id(1) == 0)
    def _():
        slope = sc_ref[h] * LOG2E
        kr = lax.broadcasted_iota(jnp.int32, (TK_DA, TQ_DA), 0)
        qc = lax.broadcasted_iota(jnp.int32, (TK_DA, TQ_DA), 1)
        for t in range(nq + nk - 1):
            bias_ref[t] = slope * jnp.abs(kr - qc + (t - (nq - 1)) * TQ_DA).astype(F32)

    lv = lamv_ref[...]
    lam = (jnp.exp(jnp.sum(lv[0:1] * lv[1:2], axis=-1, keepdims=True))
           - jnp.exp(jnp.sum(lv[2:3] * lv[3:4], axis=-1, keepdims=True)) + lam_init)
    lane = lax.broadcasted_iota(jnp.int32, (1, LANES), 1)
    g = g_ref[...]
    row0 = pl.multiple_of(jnp.minimum(pl.program_id(1), 0), 8)
    nt = (((1,), (1,)), ((), ()))

    def rows_of(blk):
        if isinstance(blk, int):
            return slice(blk * TQ_DA, (blk + 1) * TQ_DA)
        return pl.ds(pl.multiple_of(blk * TQ_DA, TQ_DA), TQ_DA)

    def scores(blk, s_ref):
        q = q_ref[0, rows_of(blk), :]
        zero = jnp.zeros_like(q)
        qst = jnp.concatenate([jnp.where(lane < DA_HEAD_DIM, q, zero),
                               jnp.where(lane < DA_HEAD_DIM, zero, q)], axis=0)
        bias = jnp.concatenate([bias_ref[j - blk + nq - 1] for j in range(nk)], axis=0)
        s = lax.dot_general(k_ref[0], qst, nt, preferred_element_type=F32)
        s = s - jnp.concatenate([bias, bias], axis=1)
        s_ref[...] = s
        return jnp.max(s, axis=0, keepdims=True)

    def attend(s_ref, m, pv_ref):
        e = jnp.exp2(s_ref[pl.ds(row0, seq), :] - m)
        pv_ref[...] = jnp.dot(vt_ref[...], e.astype(BF16), preferred_element_type=F32)

    def finalize(blk, pv_ref):
        pv, l = pv_ref[0:LANES, :], pv_ref[LANES:LANES + 1, :]
        ot = pv[:, :TQ_DA] * (1.0 / l[:, :TQ_DA]) - pv[:, TQ_DA:] * (lam / l[:, TQ_DA:])
        ot = ot * lax.rsqrt(jnp.mean(ot * ot, axis=0, keepdims=True) + RMS_EPS) * g * out_scale
        o_ref[0, rows_of(blk), :] = ot.T.astype(BF16)

    def pair(p, m_even, first=False, last=False):
        if not first:
            finalize(2 * p - 1, pb_ref)
        m_odd = scores(2 * p + 1, sb_ref)
        attend(sa_ref, m_even, pa_ref)
        m_next = None if last else scores(2 * p + 2, sa_ref)
        finalize(2 * p, pa_ref)
        attend(sb_ref, m_odd, pb_ref)
        return m_next

    m_even = pair(0, scores(0, sa_ref), first=True)
    for p in range(1, nq // 2 - 1):
        m_even = pair(p, m_even)
    pair(nq // 2 - 1, m_even, last=True)
    finalize(nq - 1, pb_ref)


def _diff_attention(z3, vt, scalars, lamv, subln_g):
    b, s, _ = z3.shape
    smem = pl.BlockSpec(memory_space=pltpu.SMEM)
    nq, nk = s // TQ_DA, s // TK_DA
    return pl.pallas_call(
        functools.partial(_da_kernel, seq=s),
        grid=(DA_HEADS, b),
        in_specs=[
            smem,
            pl.BlockSpec((4, DA_HEAD_DIM), lambda h, i: (0, 0)),
            pl.BlockSpec((2 * DA_HEAD_DIM, 1), lambda h, i: (0, 0)),
            pl.BlockSpec((1, s, LANES), lambda h, i: (i, 0, QD_BLK + h)),
            pl.BlockSpec((1, s, LANES), lambda h, i: (i, 0, KD_BLK + h)),
            pl.BlockSpec((VT_ROWS, s), lambda h, i: (h, i)),
        ],
        out_specs=pl.BlockSpec((1, s, LANES), lambda h, i: (i, 0, h)),
        out_shape=jax.ShapeDtypeStruct((b, s, DA_HEADS * LANES), BF16),
        scratch_shapes=[
            pltpu.VMEM((nq + nk - 1, TK_DA, TQ_DA), F32),
            pltpu.VMEM((s, 2 * TQ_DA), F32),
            pltpu.VMEM((s, 2 * TQ_DA), F32),
            pltpu.VMEM((LANES + SUM_ROWS, 2 * TQ_DA), F32),
            pltpu.VMEM((LANES + SUM_ROWS, 2 * TQ_DA), F32),
        ],
        compiler_params=pltpu.CompilerParams(
            dimension_semantics=("arbitrary", "arbitrary"), vmem_limit_bytes=VMEM_LIMIT),
        name="diff_attn",
    )(scalars, lamv, subln_g, z3, z3, vt)


def _na_block_geometry(rows):
    kr = min(NA_ROWS_MAX, rows)
    nblk = rows // NA_QROWS
    geo = []
    for rb in range(nblk):
        ks = min(max(NA_QROWS * rb - kr // 2, 0), rows - NA_KROWS)
        var = 0 if rb == 0 else (2 if rb == nblk - 1 else 1)
        geo.append((ks, var))
    return geo


def _na_table_dr(rows):
    kr = min(NA_ROWS_MAX, rows)
    geo = _na_block_geometry(rows)
    rep = {0: 0, 1: 1, 2: len(geo) - 1}
    out = {}
    for var, rb in rep.items():
        ks = geo[rb][0]
        for ri in range(NA_QROWS):
            qr = NA_QROWS * rb + ri
            rs = min(max(qr - kr // 2, 0), rows - kr)
            for kj in range(NA_KROWS):
                krow = ks + kj
                out[(var, ri, kj)] = (krow - qr + NA_ROWS_MAX - 1) if rs <= krow < rs + kr else None
    return out


def _build_na_table(w_ref, tab_ref, dict_ref, rows):
    kc = lax.broadcasted_iota(jnp.int32, (GRID_W, GRID_W), 0)
    qc = lax.broadcasted_iota(jnp.int32, (GRID_W, GRID_W), 1)
    qcs = jnp.clip(qc - NA_COLS // 2, 0, GRID_W - NA_COLS)
    col_valid = (kc >= qcs) & (kc < qcs + NA_COLS)
    neg = jnp.full((GRID_W, GRID_W), -jnp.inf, F32)
    sub = lax.broadcasted_iota(jnp.int32, (SUBLANES, LANES), 0)
    for dr in range(N_RPB_DR):
        rowb = jnp.broadcast_to(w_ref[0, dr:dr + 1, :], (SUBLANES, LANES))
        w8 = rowb
        for r in range(1, SUBLANES):
            w8 = jnp.where(sub == r, pltpu.roll(rowb, r, 1), w8)
        t = jnp.concatenate([pltpu.roll(w8, (kc0 - (NA_COLS - 1)) % LANES, 1)
                             for kc0 in range(0, GRID_W, SUBLANES)], axis=0)[:, :GRID_W]
        dict_ref[dr] = jnp.where(col_valid, t * LOG2E, neg)
    for (var, ri, kj), dr in _na_table_dr(rows).items():
        tile = neg if dr is None else dict_ref[dr]
        tab_ref[0, 0, var, kj * GRID_W:(kj + 1) * GRID_W, ri * GRID_W:(ri + 1) * GRID_W] = tile


def _na_kernel(tab_ref, q_ref, k_ref, vt_ref, o_ref, sa_ref, sb_ref, *, rows):
    s_refs = (sa_ref, sb_ref)
    lane = lax.broadcasted_iota(jnp.int32, (1, LANES), 1)
    vdim = lax.broadcasted_iota(jnp.int32, (LANES, 1), 0)
    nt = (((1,), (1,)), ((), ()))
    geo = _na_block_geometry(rows)
    units = []
    for rb, (ks, _) in enumerate(geo):
        if units and geo[units[-1][0]][0] == ks:
            units[-1].append(rb)
        else:
            units.append([rb])
    assert max(len(unit) for unit in units) <= NA_MAX_MERGE

    def kslice(unit):
        ks = geo[unit[0]][0]
        return slice(ks * GRID_W, ks * GRID_W + NA_TK)

    row0 = pl.multiple_of(jnp.minimum(pl.program_id(1), 0), 8)

    def scores(u):
        unit = units[u]
        pw = len(unit) * 2 * NA_TQ
        blocks, tabs = [], []
        for g in range(NA_GROUP):
            parts = []
            for rb in unit:
                q = q_ref[0, rb * NA_TQ:(rb + 1) * NA_TQ, g * LANES:(g + 1) * LANES]
                zero = jnp.zeros_like(q)
                parts += [jnp.where(lane < NA_HEAD_DIM, q, zero), jnp.where(lane < NA_HEAD_DIM, zero, q)]
                tabs.append(tab_ref[0, g, geo[rb][1]])
            qg = jnp.concatenate(parts, axis=0)
            zg = jnp.zeros_like(qg)
            blocks.append(jnp.concatenate([qg if gg == g else zg for gg in range(NA_GROUP)], axis=1))
        qbd = jnp.concatenate(blocks, axis=0)
        s = lax.dot_general(k_ref[0, kslice(unit), :], qbd, nt, preferred_element_type=F32)
        s = s + jnp.concatenate(tabs, axis=1)
        s_refs[u % 2][:, 0:NA_GROUP * pw] = s
        return jnp.max(s, axis=0, keepdims=True)

    m = scores(0)
    for u, unit in enumerate(units):
        pw = len(unit) * 2 * NA_TQ
        m_next = scores(u + 1) if u + 1 < len(units) else None
        e = jnp.exp2(s_refs[u % 2][pl.ds(row0, NA_TK), 0:NA_GROUP * pw] - m).astype(BF16)
        for g in range(NA_GROUP):
            pv = jnp.dot(vt_ref[g * VT_ROWS:(g + 1) * VT_ROWS, kslice(unit)], e[:, g * pw:(g + 1) * pw],
                         preferred_element_type=F32)
            on = pv[:LANES] * (1.0 / pv[LANES:LANES + 1])
            for n, rb in enumerate(unit):
                h0 = on[:, 2 * n * NA_TQ:(2 * n + 1) * NA_TQ]
                h1 = on[:, (2 * n + 1) * NA_TQ:(2 * n + 2) * NA_TQ]
                ot = jnp.where(vdim < NA_HEAD_DIM, h0, h1)
                o_ref[0, rb * NA_TQ:(rb + 1) * NA_TQ, g * LANES:(g + 1) * LANES] = ot.T.astype(BF16)
        m = m_next


def _neigh_attention(z3, vt, tables, layer):
    b, s, _ = z3.shape
    rows = s // GRID_W
    gw = NA_GROUP * LANES
    return pl.pallas_call(
        functools.partial(_na_kernel, rows=rows),
        grid=(NA_HEADS // 2 // NA_GROUP, b),
        in_specs=[
            pl.BlockSpec((1, NA_GROUP, 3, NA_TK, 2 * NA_TQ), lambda hp, i: (layer, hp, 0, 0, 0)),
            pl.BlockSpec((1, s, gw), lambda hp, i: (i, 0, QN_BLK // NA_GROUP + hp)),
            pl.BlockSpec((1, s, gw), lambda hp, i: (i, 0, KN_BLK // NA_GROUP + hp)),
            pl.BlockSpec((NA_GROUP * VT_ROWS, s), lambda hp, i: (VT_NA_GROUP0 // NA_GROUP + hp, i)),
        ],
        out_specs=pl.BlockSpec((1, s, gw), lambda hp, i: (i, 0, hp)),
        out_shape=jax.ShapeDtypeStruct((b, s, NA_HEADS * NA_HEAD_DIM), BF16),
        scratch_shapes=[pltpu.VMEM((NA_TK, NA_GROUP * NA_MAX_MERGE * 2 * NA_TQ), F32)] * 2,
        compiler_params=pltpu.CompilerParams(
            dimension_semantics=("arbitrary", "arbitrary"), vmem_limit_bytes=VMEM_LIMIT_NA),
        name="neigh_attn",
    )(tables, z3, z3, vt)


def _gated_out(x_ref, oda_ref, ona_ref, gate_ref, w_ref):
    gate = gate_ref[...].astype(F32)
    o = jnp.concatenate([oda_ref[...], ona_ref[...]], axis=-1).astype(F32) * (gate * jax.nn.sigmoid(gate))
    return x_ref[...] + jnp.dot(o, w_ref[...], preferred_element_type=F32)


def _outproj_final_kernel(x_ref, oda_ref, ona_ref, gate_ref, w_ref, fg_ref, xo_ref):
    xo_ref[...] = _rms(_gated_out(x_ref, oda_ref, ona_ref, gate_ref, w_ref), fg_ref[...])


def _layer_boundary_kernel(x_ref, oda_ref, ona_ref, gate_ref, wout_ref, g_ref, win_ref,
                           xo_ref, z_ref, vt_ref):
    xo = _gated_out(x_ref, oda_ref, ona_ref, gate_ref, wout_ref)
    xo_ref[...] = xo
    _project_in(xo, g_ref, win_ref, z_ref, vt_ref)


def _token_specs(layer, tm=TM_PROJ):
    half = D_MODEL // 2
    return [
        pl.BlockSpec((tm, D_MODEL), lambda i: (i, 0)),
        pl.BlockSpec((tm, half), lambda i: (i, 0)),
        pl.BlockSpec((tm, half), lambda i: (i, 0)),
        pl.BlockSpec((tm, D_MODEL), lambda i: (i, GATE_BLK)),
        _resident((D_MODEL, D_MODEL), layer),
    ]


def _outproj_final(x2, oda2, ona2, z2, w_out, layer, final_g):
    m = x2.shape[0]
    return pl.pallas_call(
        _outproj_final_kernel,
        grid=(m // TM_FINAL,),
        in_specs=_token_specs(layer, TM_FINAL) + [pl.BlockSpec((1, D_MODEL), lambda i: (0, 0))],
        out_specs=pl.BlockSpec((TM_FINAL, D_MODEL), lambda i: (i, 0)),
        out_shape=jax.ShapeDtypeStruct((m, D_MODEL), F32),
        compiler_params=pltpu.CompilerParams(
            dimension_semantics=("arbitrary",), vmem_limit_bytes=VMEM_LIMIT),
        name="outproj_final",
    )(x2, oda2, ona2, z2, w_out, final_g)


def _layer_boundary(x2, oda2, ona2, z2, w_out, norm_g, w_in, layer):
    m = x2.shape[0]
    return pl.pallas_call(
        _layer_boundary_kernel,
        grid=(m // TM_PROJ,),
        in_specs=_token_specs(layer) + [_resident((1, D_MODEL), layer + 1),
                                        _resident((D_MODEL, IN_WIDTH), layer + 1)],
        out_specs=[pl.BlockSpec((TM_PROJ, D_MODEL), lambda i: (i, 0))] + _z_out_specs(),
        out_shape=[jax.ShapeDtypeStruct((m, D_MODEL), F32)] + _z_out_shapes(m),
        compiler_params=pltpu.CompilerParams(
            dimension_semantics=("arbitrary",), vmem_limit_bytes=VMEM_LIMIT_BOUNDARY),
        name="layer_boundary",
    )(x2, oda2, ona2, z2, w_out, norm_g, w_in)


def kernel(x, norm_g, w_in, w_out, lam_q1, lam_k1, lam_q2, lam_k2, subln_g, rpb, final_g):
    b, s, d = x.shape
    depth = norm_g.shape[0]
    assert d == D_MODEL and s % TQ_DA == 0 and s % (GRID_W * NA_QROWS) == 0
    rows = s // GRID_W
    assert rows >= NA_KROWS
    m = b * s
    slopes = np.asarray(2.0 ** (-8.0 * np.arange(1, DA_HEADS + 1) / DA_HEADS), dtype=np.float32)
    fg = final_g.reshape(1, d)
    norm_g3 = norm_g.reshape(depth, 1, d)
    x2 = x.reshape(m, d)
    z2, vt, tables = _inproj(x2, norm_g3, w_in, rpb, rows)
    for l in range(depth):
        lam_init = 0.8 - 0.6 * math.exp(-0.3 * l)
        scalars = jnp.asarray(np.concatenate(
            [slopes, np.asarray([lam_init, 1.0 - lam_init, 0.0, 0.0], np.float32)]))
        lamv = jnp.stack([lam_q1[l], lam_k1[l], lam_q2[l], lam_k2[l]])
        z3 = z2.reshape(b, s, IN_WIDTH)
        oda = _diff_attention(z3, vt, scalars, lamv, subln_g[l].reshape(-1, 1)).reshape(m, -1)
        ona = _neigh_attention(z3, vt, tables, l).reshape(m, -1)
        if l + 1 < depth:
            x2, z2, vt = _layer_boundary(x2, oda, ona, z2, w_out, norm_g3, w_in, l)
        else:
            x2 = _outproj_final(x2, oda, ona, z2, w_out, l, fg)
    return x2.reshape(b, s, d)
```

```python
import functools
import math

import numpy as np
import jax
import jax.numpy as jnp
from jax import lax
from jax.experimental import pallas as pl
from jax.experimental.pallas import tpu as pltpu

F32 = jnp.float32
BF16 = jnp.bfloat16

D_MODEL = 1024
GRID_W = 64
DA_HEADS = 4
DA_HEAD_DIM = 64
NA_HEADS = 8
NA_HEAD_DIM = 64
NA_ROWS_MAX = 8
NA_COLS = 16
RMS_EPS = 1e-6
IN_WIDTH = 4096
LANES = 128
SUBLANES = 8

QD_BLK, KD_BLK, VD_BLK = 0, 4, 8
QN_BLK, KN_BLK, VN_BLK = 12, 16, 20
GATE_BLK = 3

TM_PROJ = 512
TM_FINAL = 1024
TQ_DA = 256
TK_DA = TQ_DA
SUM_ROWS = 16
NA_QROWS = 4
NA_KROWS = 12
NA_MAX_MERGE = 2
NA_GROUP = 2
assert QN_BLK % NA_GROUP == 0 and KN_BLK % NA_GROUP == 0
NA_TQ = NA_QROWS * GRID_W
NA_TK = NA_KROWS * GRID_W
N_RPB_DR = 2 * NA_ROWS_MAX - 1
N_RPB_DC = 2 * NA_COLS - 1
VMEM_LIMIT = 48 * 1024 * 1024
VMEM_LIMIT_BOUNDARY = 56 * 1024 * 1024
VMEM_LIMIT_NA = 56 * 1024 * 1024
LOG2E = math.log2(math.e)
Q_SCALE = DA_HEAD_DIM ** -0.5 * LOG2E
assert NA_HEAD_DIM == DA_HEAD_DIM
Z_CHUNK = 512
Q_CHUNKS = (QD_BLK * LANES // Z_CHUNK, QN_BLK * LANES // Z_CHUNK)
V_CHUNKS = (VD_BLK * LANES // Z_CHUNK, VN_BLK * LANES // Z_CHUNK)
VT_ROWS = LANES + SUM_ROWS
VT_GROUPS = len(V_CHUNKS) * Z_CHUNK // LANES
VT_NA_GROUP0 = Z_CHUNK // LANES


def _rms(x, g):
    return x * lax.rsqrt(jnp.mean(x * x, axis=-1, keepdims=True) + RMS_EPS) * g


def _project_in(x, g_ref, w_ref, z_ref, vt_ref):
    h = _rms(x, g_ref[...])
    ones = jnp.ones((SUM_ROWS, x.shape[0]), BF16)
    for j in range(IN_WIDTH // Z_CHUNK):
        cols = slice(j * Z_CHUNK, (j + 1) * Z_CHUNK)
        zc = jnp.dot(h, w_ref[:, cols], preferred_element_type=F32)
        if j in Q_CHUNKS:
            zc = zc * Q_SCALE
        z_ref[:, cols] = zc.astype(BF16)
        if j in V_CHUNKS:
            zt = zc.T.astype(BF16)
            for t in range(Z_CHUNK // LANES):
                grp = V_CHUNKS.index(j) * (Z_CHUNK // LANES) + t
                vt_ref[grp * VT_ROWS:grp * VT_ROWS + LANES, :] = zt[t * LANES:(t + 1) * LANES]
                vt_ref[grp * VT_ROWS + LANES:(grp + 1) * VT_ROWS, :] = ones


def _inproj_kernel(x_ref, g_ref, w_ref, rpb_ref, z_ref, vt_ref, tab_ref, dict_ref, *, rows):
    _project_in(x_ref[...], g_ref, w_ref, z_ref, vt_ref)
    _build_na_table(rpb_ref, tab_ref, dict_ref, rows)


def _resident(shape, layer):
    return pl.BlockSpec((None,) + shape, lambda i: (layer,) + (0,) * len(shape),
                        pipeline_mode=pl.Buffered(1))


def _inproj(x2, norm_g, w_in, rpb, rows):
    m = x2.shape[0]
    depth = rpb.shape[0]
    steps = m // TM_PROJ
    assert steps == depth * NA_HEADS, "one (layer, head) bias table is built per projection step"
    rpb_rev = jnp.pad(rpb[..., ::-1], ((0, 0), (0, 0), (0, 0), (0, LANES - N_RPB_DC)))
    rpb_rev = rpb_rev.reshape(depth * NA_HEADS, N_RPB_DR, LANES)
    tab_spec = pl.BlockSpec((1, 1, 3, NA_TK, NA_TQ),
                            lambda i: (i // NA_HEADS, (i % NA_HEADS) // 2, 0, 0, i % 2))
    tab_shape = jax.ShapeDtypeStruct((depth, NA_HEADS // 2, 3, NA_TK, 2 * NA_TQ), F32)
    return pl.pallas_call(
        functools.partial(_inproj_kernel, rows=rows),
        grid=(steps,),
        in_specs=[
            pl.BlockSpec((TM_PROJ, D_MODEL), lambda i: (i, 0)),
            _resident((1, D_MODEL), 0),
            _resident((D_MODEL, IN_WIDTH), 0),
            pl.BlockSpec((1, N_RPB_DR, LANES), lambda i: (i, 0, 0)),
        ],
        out_specs=_z_out_specs() + [tab_spec],
        out_shape=_z_out_shapes(m) + [tab_shape],
        scratch_shapes=[pltpu.VMEM((N_RPB_DR, GRID_W, GRID_W), F32)],
        compiler_params=pltpu.CompilerParams(
            dimension_semantics=("arbitrary",), vmem_limit_bytes=VMEM_LIMIT),
        name="inproj",
    )(x2, norm_g, w_in, rpb_rev)


def _z_out_specs():
    return [pl.BlockSpec((TM_PROJ, IN_WIDTH), lambda i: (i, 0)),
            pl.BlockSpec((VT_GROUPS * VT_ROWS, TM_PROJ), lambda i: (0, i))]


def _z_out_shapes(m):
    return [jax.ShapeDtypeStruct((m, IN_WIDTH), BF16),
            jax.ShapeDtypeStruct((VT_GROUPS * VT_ROWS, m), BF16)]


def _da_kernel(sc_ref, lamv_ref, g_ref, q_ref, k_ref, vt_ref, o_ref,
               bias_ref, sa_ref, sb_ref, pa_ref, pb_ref, *, seq):
    nq = seq // TQ_DA
    nk = seq // TK_DA
    h = pl.program_id(0)
    lam_init = sc_ref[4]
    out_scale = sc_ref[5]

    @pl.when(pl.program_id(1) == 0)
    def _():
        slope = sc_ref[h] * LOG2E
        kr = lax.broadcasted_iota(jnp.int32, (TK_DA, TQ_DA), 0)
        qc = lax.broadcasted_iota(jnp.int32, (TK_DA, TQ_DA), 1)
        for t in range(nq + nk - 1):
            bias_ref[t] = slope * jnp.abs(kr - qc + (t - (nq - 1)) * TQ_DA).astype(F32)

    lv = lamv_ref[...]
    lam = (jnp.exp(jnp.sum(lv[0:1] * lv[1:2], axis=-1, keepdims=True))
           - jnp.exp(jnp.sum(lv[2:3] * lv[3:4], axis=-1, keepdims=True)) + lam_init)
    lane = lax.broadcasted_iota(jnp.int32, (1, LANES), 1)
    g = g_ref[...]
    row0 = pl.multiple_of(jnp.minimum(pl.program_id(1), 0), 8)
    nt = (((1,), (1,)), ((), ()))

    def rows_of(blk):
        if isinstance(blk, int):
            return slice(blk * TQ_DA, (blk + 1) * TQ_DA)
        return pl.ds(pl.multiple_of(blk * TQ_DA, TQ_DA), TQ_DA)

    def scores(blk, s_ref):
        q = q_ref[0, rows_of(blk), :]
        zero = jnp.zeros_like(q)
        qst = jnp.concatenate([jnp.where(lane < DA_HEAD_DIM, q, zero),
                               jnp.where(lane < DA_HEAD_DIM, zero, q)], axis=0)
        bias = jnp.concatenate([bias_ref[j - blk + nq - 1] for j in range(nk)], axis=0)
        s = lax.dot_general(k_ref[0], qst, nt, preferred_element_type=F32)
        s = s - jnp.concatenate([bias, bias], axis=1)
        s_ref[...] = s
        return jnp.max(s, axis=0, keepdims=True)

    def attend(s_ref, m, pv_ref):
        e = jnp.exp2(s_ref[pl.ds(row0, seq), :] - m)
        pv_ref[...] = jnp.dot(vt_ref[...], e.astype(BF16), preferred_element_type=F32)

    def finalize(blk, pv_ref):
        pv, l = pv_ref[0:LANES, :], pv_ref[LANES:LANES + 1, :]
        ot = pv[:, :TQ_DA] * (1.0 / l[:, :TQ_DA]) - pv[:, TQ_DA:] * (lam / l[:, TQ_DA:])
        ot = ot * lax.rsqrt(jnp.mean(ot * ot, axis=0, keepdims=True) + RMS_EPS) * g * out_scale
        o_ref[0, rows_of(blk), :] = ot.T.astype(BF16)

    def pair(p, m_even, first=False, last=False):
        if not first:
            finalize(2 * p - 1, pb_ref)
        m_odd = scores(2 * p + 1, sb_ref)
        attend(sa_ref, m_even, pa_ref)
        m_next = None if last else scores(2 * p + 2, sa_ref)
        finalize(2 * p, pa_ref)
        attend(sb_ref, m_odd, pb_ref)
        return m_next

    m_even = pair(0, scores(0, sa_ref), first=True)
    for p in range(1, nq // 2 - 1):
        m_even = pair(p, m_even)
    pair(nq // 2 - 1, m_even, last=True)
    finalize(nq - 1, pb_ref)


def _diff_attention(z3, vt, scalars, lamv, subln_g):
    b, s, _ = z3.shape
    smem = pl.BlockSpec(memory_space=pltpu.SMEM)
    nq, nk = s // TQ_DA, s // TK_DA
    return pl.pallas_call(
        functools.partial(_da_kernel, seq=s),
        grid=(DA_HEADS, b),
        in_specs=[
            smem,
            pl.BlockSpec((4, DA_HEAD_DIM), lambda h, i: (0, 0)),
            pl.BlockSpec((2 * DA_HEAD_DIM, 1), lambda h, i: (0, 0)),
            pl.BlockSpec((1, s, LANES), lambda h, i: (i, 0, QD_BLK + h)),
            pl.BlockSpec((1, s, LANES), lambda h, i: (i, 0, KD_BLK + h)),
            pl.BlockSpec((VT_ROWS, s), lambda h, i: (h, i)),
        ],
        out_specs=pl.BlockSpec((1, s, LANES), lambda h, i: (i, 0, h)),
        out_shape=jax.ShapeDtypeStruct((b, s, DA_HEADS * LANES), BF16),
        scratch_shapes=[
            pltpu.VMEM((nq + nk - 1, TK_DA, TQ_DA), F32),
            pltpu.VMEM((s, 2 * TQ_DA), F32),
            pltpu.VMEM((s, 2 * TQ_DA), F32),
            pltpu.VMEM((LANES + SUM_ROWS, 2 * TQ_DA), F32),
            pltpu.VMEM((LANES + SUM_ROWS, 2 * TQ_DA), F32),
        ],
        compiler_params=pltpu.CompilerParams(
            dimension_semantics=("arbitrary", "arbitrary"), vmem_limit_bytes=VMEM_LIMIT),
        name="diff_attn",
    )(scalars, lamv, subln_g, z3, z3, vt)


def _na_block_geometry(rows):
    kr = min(NA_ROWS_MAX, rows)
    nblk = rows // NA_QROWS
    geo = []
    for rb in range(nblk):
        ks = min(max(NA_QROWS * rb - kr // 2, 0), rows - NA_KROWS)
        var = 0 if rb == 0 else (2 if rb == nblk - 1 else 1)
        geo.append((ks, var))
    return geo


def _na_table_dr(rows):
    kr = min(NA_ROWS_MAX, rows)
    geo = _na_block_geometry(rows)
    rep = {0: 0, 1: 1, 2: len(geo) - 1}
    out = {}
    for var, rb in rep.items():
        ks = geo[rb][0]
        for ri in range(NA_QROWS):
            qr = NA_QROWS * rb + ri
            rs = min(max(qr - kr // 2, 0), rows - kr)
            for kj in range(NA_KROWS):
                krow = ks + kj
                out[(var, ri, kj)] = (krow - qr + NA_ROWS_MAX - 1) if rs <= krow < rs + kr else None
    return out


def _build_na_table(w_ref, tab_ref, dict_ref, rows):
    kc = lax.broadcasted_iota(jnp.int32, (GRID_W, GRID_W), 0)
    qc = lax.broadcasted_iota(jnp.int32, (GRID_W, GRID_W), 1)
    qcs = jnp.clip(qc - NA_COLS // 2, 0, GRID_W - NA_COLS)
    col_valid = (kc >= qcs) & (kc < qcs + NA_COLS)
    neg = jnp.full((GRID_W, GRID_W), -jnp.inf, F32)
    sub = lax.broadcasted_iota(jnp.int32, (SUBLANES, LANES), 0)
    for dr in range(N_RPB_DR):
        rowb = jnp.broadcast_to(w_ref[0, dr:dr + 1, :], (SUBLANES, LANES))
        w8 = rowb
        for r in range(1, SUBLANES):
            w8 = jnp.where(sub == r, pltpu.roll(rowb, r, 1), w8)
        t = jnp.concatenate([pltpu.roll(w8, (kc0 - (NA_COLS - 1)) % LANES, 1)
                             for kc0 in range(0, GRID_W, SUBLANES)], axis=0)[:, :GRID_W]
        dict_ref[dr] = jnp.where(col_valid, t * LOG2E, neg)
    for (var, ri, kj), dr in _na_table_dr(rows).items():
        tile = neg if dr is None else dict_ref[dr]
        tab_ref[0, 0, var, kj * GRID_W:(kj + 1) * GRID_W, ri * GRID_W:(ri + 1) * GRID_W] = tile


def _na_kernel(tab_ref, q_ref, k_ref, vt_ref, o_ref, sa_ref, sb_ref, *, rows):
    s_refs = (sa_ref, sb_ref)
    lane = lax.broadcasted_iota(jnp.int32, (1, LANES), 1)
    vdim = lax.broadcasted_iota(jnp.int32, (LANES, 1), 0)
    nt = (((1,), (1,)), ((), ()))
    geo = _na_block_geometry(rows)
    units = []
    for rb, (ks, _) in enumerate(geo):
        if units and geo[units[-1][0]][0] == ks:
            units[-1].append(rb)
        else:
            units.append([rb])
    assert max(len(unit) for unit in units) <= NA_MAX_MERGE

    def kslice(unit):
        ks = geo[unit[0]][0]
        return slice(ks * GRID_W, ks * GRID_W + NA_TK)

    row0 = pl.multiple_of(jnp.minimum(pl.program_id(1), 0), 8)

    def scores(u):
        unit = units[u]
        pw = len(unit) * 2 * NA_TQ
        blocks, tabs = [], []
        for g in range(NA_GROUP):
            parts = []
            for rb in unit:
                q = q_ref[0, rb * NA_TQ:(rb + 1) * NA_TQ, g * LANES:(g + 1) * LANES]
                zero = jnp.zeros_like(q)
                parts += [jnp.where(lane < NA_HEAD_DIM, q, zero), jnp.where(lane < NA_HEAD_DIM, zero, q)]
                tabs.append(tab_ref[0, g, geo[rb][1]])
            qg = jnp.concatenate(parts, axis=0)
            zg = jnp.zeros_like(qg)
            blocks.append(jnp.concatenate([qg if gg == g else zg for gg in range(NA_GROUP)], axis=1))
        qbd = jnp.concatenate(blocks, axis=0)
        s = lax.dot_general(k_ref[0, kslice(unit), :], qbd, nt, preferred_element_type=F32)
        s = s + jnp.concatenate(tabs, axis=1)
        s_refs[u % 2][:, 0:NA_GROUP * pw] = s
        return jnp.max(s, axis=0, keepdims=True)

    m = scores(0)
    for u, unit in enumerate(units):
        pw = len(unit) * 2 * NA_TQ
        m_next = scores(u + 1) if u + 1 < len(units) else None
        e = jnp.exp2(s_refs[u % 2][pl.ds(row0, NA_TK), 0:NA_GROUP * pw] - m).astype(BF16)
        for g in range(NA_GROUP):
            pv = jnp.dot(vt_ref[g * VT_ROWS:(g + 1) * VT_ROWS, kslice(unit)], e[:, g * pw:(g + 1) * pw],
                         preferred_element_type=F32)
            on = pv[:LANES] * (1.0 / pv[LANES:LANES + 1])
            for n, rb in enumerate(unit):
                h0 = on[:, 2 * n * NA_TQ:(2 * n + 1) * NA_TQ]
                h1 = on[:, (2 * n + 1) * NA_TQ:(2 * n + 2) * NA_TQ]
                ot = jnp.where(vdim < NA_HEAD_DIM, h0, h1)
                o_ref[0, rb * NA_TQ:(rb + 1) * NA_TQ, g * LANES:(g + 1) * LANES] = ot.T.astype(BF16)
        m = m_next


def _neigh_attention(z3, vt, tables, layer):
    b, s, _ = z3.shape
    rows = s // GRID_W
    gw = NA_GROUP * LANES
    return pl.pallas_call(
        functools.partial(_na_kernel, rows=rows),
        grid=(NA_HEADS // 2 // NA_GROUP, b),
        in_specs=[
            pl.BlockSpec((1, NA_GROUP, 3, NA_TK, 2 * NA_TQ), lambda hp, i: (layer, hp, 0, 0, 0)),
            pl.BlockSpec((1, s, gw), lambda hp, i: (i, 0, QN_BLK // NA_GROUP + hp)),
            pl.BlockSpec((1, s, gw), lambda hp, i: (i, 0, KN_BLK // NA_GROUP + hp)),
            pl.BlockSpec((NA_GROUP * VT_ROWS, s), lambda hp, i: (VT_NA_GROUP0 // NA_GROUP + hp, i)),
        ],
        out_specs=pl.BlockSpec((1, s, gw), lambda hp, i: (i, 0, hp)),
        out_shape=jax.ShapeDtypeStruct((b, s, NA_HEADS * NA_HEAD_DIM), BF16),
        scratch_shapes=[pltpu.VMEM((NA_TK, NA_GROUP * NA_MAX_MERGE * 2 * NA_TQ), F32)] * 2,
        compiler_params=pltpu.CompilerParams(
            dimension_semantics=("arbitrary", "arbitrary"), vmem_limit_bytes=VMEM_LIMIT_NA),
        name="neigh_attn",
    )(tables, z3, z3, vt)


def _gated_out(x_ref, oda_ref, ona_ref, gate_ref, w_ref):
    gate = gate_ref[...].astype(F32)
    o = jnp.concatenate([oda_ref[...], ona_ref[...]], axis=-1).astype(F32) * (gate * jax.nn.sigmoid(gate))
    return x_ref[...] + jnp.dot(o, w_ref[...], preferred_element_type=F32)


def _outproj_final_kernel(x_ref, oda_ref, ona_ref, gate_ref, w_ref, fg_ref, xo_ref):
    xo_ref[...] = _rms(_gated_out(x_ref, oda_ref, ona_ref, gate_ref, w_ref), fg_ref[...])


def _layer_boundary_kernel(x_ref, oda_ref, ona_ref, gate_ref, wout_ref, g_ref, win_ref,
                           xo_ref, z_ref, vt_ref):
    xo = _gated_out(x_ref, oda_ref, ona_ref, gate_ref, wout_ref)
    xo_ref[...] = xo
    _project_in(xo, g_ref, win_ref, z_ref, vt_ref)


def _token_specs(layer, tm=TM_PROJ):
    half = D_MODEL // 2
    return [
        pl.BlockSpec((tm, D_MODEL), lambda i: (i, 0)),
        pl.BlockSpec((tm, half), lambda i: (i, 0)),
        pl.BlockSpec((tm, half), lambda i: (i, 0)),
        pl.BlockSpec((tm, D_MODEL), lambda i: (i, GATE_BLK)),
        _resident((D_MODEL, D_MODEL), layer),
    ]


def _outproj_final(x2, oda2, ona2, z2, w_out, layer, final_g):
    m = x2.shape[0]
    return pl.pallas_call(
        _outproj_final_kernel,
        grid=(m // TM_FINAL,),
        in_specs=_token_specs(layer, TM_FINAL) + [pl.BlockSpec((1, D_MODEL), lambda i: (0, 0))],
        out_specs=pl.BlockSpec((TM_FINAL, D_MODEL), lambda i: (i, 0)),
        out_shape=jax.ShapeDtypeStruct((m, D_MODEL), F32),
        compiler_params=pltpu.CompilerParams(
            dimension_semantics=("arbitrary",), vmem_limit_bytes=VMEM_LIMIT),
        name="outproj_final",
    )(x2, oda2, ona2, z2, w_out, final_g)


def _layer_boundary(x2, oda2, ona2, z2, w_out, norm_g, w_in, layer):
    m = x2.shape[0]
    return pl.pallas_call(
        _layer_boundary_kernel,
        grid=(m // TM_PROJ,),
        in_specs=_token_specs(layer) + [_resident((1, D_MODEL), layer + 1),
                                        _resident((D_MODEL, IN_WIDTH), layer + 1)],
        out_specs=[pl.BlockSpec((TM_PROJ, D_MODEL), lambda i: (i, 0))] + _z_out_specs(),
        out_shape=[jax.ShapeDtypeStruct((m, D_MODEL), F32)] + _z_out_shapes(m),
        compiler_params=pltpu.CompilerParams(
            dimension_semantics=("arbitrary",), vmem_limit_bytes=VMEM_LIMIT_BOUNDARY),
        name="layer_boundary",
    )(x2, oda2, ona2, z2, w_out, norm_g, w_in)


def kernel(x, norm_g, w_in, w_out, lam_q1, lam_k1, lam_q2, lam_k2, subln_g, rpb, final_g):
    b, s, d = x.shape
    depth = norm_g.shape[0]
    assert d == D_MODEL and s % TQ_DA == 0 and s % (GRID_W * NA_QROWS) == 0
    rows = s // GRID_W
    assert rows >= NA_KROWS
    m = b * s
    slopes = np.asarray(2.0 ** (-8.0 * np.arange(1, DA_HEADS + 1) / DA_HEADS), dtype=np.float32)
    fg = final_g.reshape(1, d)
    norm_g3 = norm_g.reshape(depth, 1, d)
    x2 = x.reshape(m, d)
    z2, vt, tables = _inproj(x2, norm_g3, w_in, rpb, rows)
    for l in range(depth):
        lam_init = 0.8 - 0.6 * math.exp(-0.3 * l)
        scalars = jnp.asarray(np.concatenate(
            [slopes, np.asarray([lam_init, 1.0 - lam_init, 0.0, 0.0], np.float32)]))
        lamv = jnp.stack([lam_q1[l], lam_k1[l], lam_q2[l], lam_k2[l]])
        z3 = z2.reshape(b, s, IN_WIDTH)
        oda = _diff_attention(z3, vt, scalars, lamv, subln_g[l].reshape(-1, 1)).reshape(m, -1)
        ona = _neigh_attention(z3, vt, tables, l).reshape(m, -1)
        if l + 1 < depth:
            x2, z2, vt = _layer_boundary(x2, oda, ona, z2, w_out, norm_g3, w_in, l)
        else:
            x2 = _outproj_final(x2, oda, ona, z2, w_out, l, fg)
    return x2.reshape(b, s, d)
```

```python
import functools
import math

import numpy as np
import jax
import jax.numpy as jnp
from jax import lax
from jax.experimental import pallas as pl
from jax.experimental.pallas import tpu as pltpu

F32 = jnp.float32
BF16 = jnp.bfloat16

D_MODEL = 1024
GRID_W = 64
DA_HEADS = 4
DA_HEAD_DIM = 64
NA_HEADS = 8
NA_HEAD_DIM = 64
NA_ROWS_MAX = 8
NA_COLS = 16
RMS_EPS = 1e-6
IN_WIDTH = 4096
LANES = 128
SUBLANES = 8

QD_BLK, KD_BLK, VD_BLK = 0, 4, 8
QN_BLK, KN_BLK, VN_BLK = 12, 16, 20
GATE_BLK = 3

TM_PROJ = 512
TM_FINAL = 1024
TQ_DA = 256
TK_DA = TQ_DA
SUM_ROWS = 16
NA_QROWS = 4
NA_KROWS = 11
NA_MAX_MERGE = 2
NA_GROUP = 2
assert QN_BLK % NA_GROUP == 0 and KN_BLK % NA_GROUP == 0
NA_TQ = NA_QROWS * GRID_W
NA_TK = NA_KROWS * GRID_W
N_RPB_DR = 2 * NA_ROWS_MAX - 1
N_RPB_DC = 2 * NA_COLS - 1
VMEM_LIMIT = 48 * 1024 * 1024
VMEM_LIMIT_BOUNDARY = 56 * 1024 * 1024
VMEM_LIMIT_NA = 56 * 1024 * 1024
LOG2E = math.log2(math.e)
Q_SCALE = DA_HEAD_DIM ** -0.5 * LOG2E
assert NA_HEAD_DIM == DA_HEAD_DIM
Z_CHUNK = 512
Q_CHUNKS = (QD_BLK * LANES // Z_CHUNK, QN_BLK * LANES // Z_CHUNK)
V_CHUNKS = (VD_BLK * LANES // Z_CHUNK, VN_BLK * LANES // Z_CHUNK)
VT_ROWS = LANES + SUM_ROWS
VT_GROUPS = len(V_CHUNKS) * Z_CHUNK // LANES
VT_NA_GROUP0 = Z_CHUNK // LANES


def _rms(x, g):
    return x * lax.rsqrt(jnp.mean(x * x, axis=-1, keepdims=True) + RMS_EPS) * g


def _project_in(x, g_ref, w_ref, z_ref, vt_ref):
    h = _rms(x, g_ref[...])
    ones = jnp.ones((SUM_ROWS, x.shape[0]), BF16)
    for j in range(IN_WIDTH // Z_CHUNK):
        cols = slice(j * Z_CHUNK, (j + 1) * Z_CHUNK)
        zc = jnp.dot(h, w_ref[:, cols], preferred_element_type=F32)
        if j in Q_CHUNKS:
            zc = zc * Q_SCALE
        z_ref[:, cols] = zc.astype(BF16)
        if j in V_CHUNKS:
            zt = zc.T.astype(BF16)
            for t in range(Z_CHUNK // LANES):
                grp = V_CHUNKS.index(j) * (Z_CHUNK // LANES) + t
                vt_ref[grp * VT_ROWS:grp * VT_ROWS + LANES, :] = zt[t * LANES:(t + 1) * LANES]
                vt_ref[grp * VT_ROWS + LANES:(grp + 1) * VT_ROWS, :] = ones


def _inproj_kernel(x_ref, g_ref, w_ref, rpb_ref, z_ref, vt_ref, tab_ref, dict_ref, *, rows):
    _project_in(x_ref[...], g_ref, w_ref, z_ref, vt_ref)
    _build_na_table(rpb_ref, tab_ref, dict_ref, rows)


def _resident(shape, layer):
    return pl.BlockSpec((None,) + shape, lambda i: (layer,) + (0,) * len(shape),
                        pipeline_mode=pl.Buffered(1))


def _inproj(x2, norm_g, w_in, rpb, rows):
    m = x2.shape[0]
    depth = rpb.shape[0]
    steps = m // TM_PROJ
    assert steps == depth * NA_HEADS, "one (layer, head) bias table is built per projection step"
    rpb_rev = jnp.pad(rpb[..., ::-1], ((0, 0), (0, 0), (0, 0), (0, LANES - N_RPB_DC)))
    rpb_rev = rpb_rev.reshape(depth * NA_HEADS, N_RPB_DR, LANES)
    tab_spec = pl.BlockSpec((1, 1, 3, NA_TK, NA_TQ),
                            lambda i: (i // NA_HEADS, (i % NA_HEADS) // 2, 0, 0, i % 2))
    tab_shape = jax.ShapeDtypeStruct((depth, NA_HEADS // 2, 3, NA_TK, 2 * NA_TQ), F32)
    return pl.pallas_call(
        functools.partial(_inproj_kernel, rows=rows),
        grid=(steps,),
        in_specs=[
            pl.BlockSpec((TM_PROJ, D_MODEL), lambda i: (i, 0)),
            _resident((1, D_MODEL), 0),
            _resident((D_MODEL, IN_WIDTH), 0),
            pl.BlockSpec((1, N_RPB_DR, LANES), lambda i: (i, 0, 0)),
        ],
        out_specs=_z_out_specs() + [tab_spec],
        out_shape=_z_out_shapes(m) + [tab_shape],
        scratch_shapes=[pltpu.VMEM((N_RPB_DR, GRID_W, GRID_W), F32)],
        compiler_params=pltpu.CompilerParams(
            dimension_semantics=("arbitrary",), vmem_limit_bytes=VMEM_LIMIT),
        name="inproj",
    )(x2, norm_g, w_in, rpb_rev)


def _z_out_specs():
    return [pl.BlockSpec((TM_PROJ, IN_WIDTH), lambda i: (i, 0)),
            pl.BlockSpec((VT_GROUPS * VT_ROWS, TM_PROJ), lambda i: (0, i))]


def _z_out_shapes(m):
    return [jax.ShapeDtypeStruct((m, IN_WIDTH), BF16),
            jax.ShapeDtypeStruct((VT_GROUPS * VT_ROWS, m), BF16)]


def _da_kernel(sc_ref, lamv_ref, g_ref, q_ref, k_ref, vt_ref, o_ref,
               bias_ref, sa_ref, sb_ref, pa_ref, pb_ref, *, seq):
    nq = seq // TQ_DA
    nk = seq // TK_DA
    h = pl.program_id(0)
    lam_init = sc_ref[4]
    out_scale = sc_ref[5]

    @pl.when(pl.program_id(1) == 0)
    def _():
        slope = sc_ref[h] * LOG2E
        kr = lax.broadcasted_iota(jnp.int32, (TK_DA, TQ_DA), 0)
        qc = lax.broadcasted_iota(jnp.int32, (TK_DA, TQ_DA), 1)
        for t in range(nq + nk - 1):
            bias_ref[t] = slope * jnp.abs(kr - qc + (t - (nq - 1)) * TQ_DA).astype(F32)

    lv = lamv_ref[...]
    lam = (jnp.exp(jnp.sum(lv[0:1] * lv[1:2], axis=-1, keepdims=True))
           - jnp.exp(jnp.sum(lv[2:3] * lv[3:4], axis=-1, keepdims=True)) + lam_init)
    lane = lax.broadcasted_iota(jnp.int32, (1, LANES), 1)
    g = g_ref[...]
    row0 = pl.multiple_of(jnp.minimum(pl.program_id(1), 0), 8)
    nt = (((1,), (1,)), ((), ()))

    def rows_of(blk):
        return slice(blk * TQ_DA, (blk + 1) * TQ_DA)

    def scores(blk, s_ref):
        q = q_ref[0, rows_of(blk), :]
        zero = jnp.zeros_like(q)
        qst = jnp.concatenate([jnp.where(lane < DA_HEAD_DIM, q, zero),
                               jnp.where(lane < DA_HEAD_DIM, zero, q)], axis=0)
        bias = jnp.concatenate([bias_ref[j - blk + nq - 1] for j in range(nk)], axis=0)
        s = lax.dot_general(k_ref[0], qst, nt, preferred_element_type=F32)
        s = s - jnp.concatenate([bias, bias], axis=1)
        s_ref[...] = s
        return jnp.max(s, axis=0, keepdims=True)

    def attend(s_ref, m, pv_ref):
        e = jnp.exp2(s_ref[pl.ds(row0, seq), :] - m)
        pv_ref[...] = jnp.dot(vt_ref[...], e.astype(BF16), preferred_element_type=F32)

    def finalize(blk, pv_ref):
        pv, l = pv_ref[0:LANES, :], pv_ref[LANES:LANES + 1, :]
        ot = pv[:, :TQ_DA] * (1.0 / l[:, :TQ_DA]) - pv[:, TQ_DA:] * (lam / l[:, TQ_DA:])
        ot = ot * lax.rsqrt(jnp.mean(ot * ot, axis=0, keepdims=True) + RMS_EPS) * g * out_scale
        o_ref[0, rows_of(blk), :] = ot.T.astype(BF16)

    def pair(p, m_even, first=False, last=False):
        if not first:
            finalize(2 * p - 1, pb_ref)
        m_odd = scores(2 * p + 1, sb_ref)
        attend(sa_ref, m_even, pa_ref)
        m_next = None if last else scores(2 * p + 2, sa_ref)
        finalize(2 * p, pa_ref)
        attend(sb_ref, m_odd, pb_ref)
        return m_next

    m_even = pair(0, scores(0, sa_ref), first=True)
    for p in range(1, nq // 2 - 1):
        m_even = pair(p, m_even)
    pair(nq // 2 - 1, m_even, last=True)
    finalize(nq - 1, pb_ref)


def _diff_attention(z3, vt, scalars, lamv, subln_g):
    b, s, _ = z3.shape
    smem = pl.BlockSpec(memory_space=pltpu.SMEM)
    nq, nk = s // TQ_DA, s // TK_DA
    return pl.pallas_call(
        functools.partial(_da_kernel, seq=s),
        grid=(DA_HEADS, b),
        in_specs=[
            smem,
            pl.BlockSpec((4, DA_HEAD_DIM), lambda h, i: (0, 0)),
            pl.BlockSpec((2 * DA_HEAD_DIM, 1), lambda h, i: (0, 0)),
            pl.BlockSpec((1, s, LANES), lambda h, i: (i, 0, QD_BLK + h)),
            pl.BlockSpec((1, s, LANES), lambda h, i: (i, 0, KD_BLK + h)),
            pl.BlockSpec((VT_ROWS, s), lambda h, i: (h, i)),
        ],
        out_specs=pl.BlockSpec((1, s, LANES), lambda h, i: (i, 0, h)),
        out_shape=jax.ShapeDtypeStruct((b, s, DA_HEADS * LANES), BF16),
        scratch_shapes=[
            pltpu.VMEM((nq + nk - 1, TK_DA, TQ_DA), F32),
            pltpu.VMEM((s, 2 * TQ_DA), F32),
            pltpu.VMEM((s, 2 * TQ_DA), F32),
            pltpu.VMEM((LANES + SUM_ROWS, 2 * TQ_DA), F32),
            pltpu.VMEM((LANES + SUM_ROWS, 2 * TQ_DA), F32),
        ],
        compiler_params=pltpu.CompilerParams(
            dimension_semantics=("arbitrary", "arbitrary"), vmem_limit_bytes=VMEM_LIMIT),
        name="diff_attn",
    )(scalars, lamv, subln_g, z3, z3, vt)


def _na_block_geometry(rows):
    kr = min(NA_ROWS_MAX, rows)
    nblk = rows // NA_QROWS
    geo = []
    for rb in range(nblk):
        ks = min(max(NA_QROWS * rb - kr // 2, 0), rows - NA_KROWS)
        var = 0 if rb == 0 else (2 if rb == nblk - 1 else 1)
        geo.append((ks, var))
    return geo


def _na_table_dr(rows):
    kr = min(NA_ROWS_MAX, rows)
    geo = _na_block_geometry(rows)
    rep = {0: 0, 1: 1, 2: len(geo) - 1}
    out = {}
    for var, rb in rep.items():
        ks = geo[rb][0]
        for ri in range(NA_QROWS):
            qr = NA_QROWS * rb + ri
            rs = min(max(qr - kr // 2, 0), rows - kr)
            for kj in range(NA_KROWS):
                krow = ks + kj
                out[(var, ri, kj)] = (krow - qr + NA_ROWS_MAX - 1) if rs <= krow < rs + kr else None
    return out


def _build_na_table(w_ref, tab_ref, dict_ref, rows):
    kc = lax.broadcasted_iota(jnp.int32, (GRID_W, GRID_W), 0)
    qc = lax.broadcasted_iota(jnp.int32, (GRID_W, GRID_W), 1)
    qcs = jnp.clip(qc - NA_COLS // 2, 0, GRID_W - NA_COLS)
    col_valid = (kc >= qcs) & (kc < qcs + NA_COLS)
    neg = jnp.full((GRID_W, GRID_W), -jnp.inf, F32)
    sub = lax.broadcasted_iota(jnp.int32, (SUBLANES, LANES), 0)
    for dr in range(N_RPB_DR):
        rowb = jnp.broadcast_to(w_ref[0, dr:dr + 1, :], (SUBLANES, LANES))
        w8 = rowb
        for r in range(1, SUBLANES):
            w8 = jnp.where(sub == r, pltpu.roll(rowb, r, 1), w8)
        t = jnp.concatenate([pltpu.roll(w8, (kc0 - (NA_COLS - 1)) % LANES, 1)
                             for kc0 in range(0, GRID_W, SUBLANES)], axis=0)[:, :GRID_W]
        dict_ref[dr] = jnp.where(col_valid, t * LOG2E, neg)
    for (var, ri, kj), dr in _na_table_dr(rows).items():
        tile = neg if dr is None else dict_ref[dr]
        tab_ref[0, 0, var, kj * GRID_W:(kj + 1) * GRID_W, ri * GRID_W:(ri + 1) * GRID_W] = tile


def _na_kernel(tab_ref, q_ref, k_ref, vt_ref, o_ref, sa_ref, sb_ref, *, rows):
    s_refs = (sa_ref, sb_ref)
    lane = lax.broadcasted_iota(jnp.int32, (1, LANES), 1)
    vdim = lax.broadcasted_iota(jnp.int32, (LANES, 1), 0)
    nt = (((1,), (1,)), ((), ()))
    geo = _na_block_geometry(rows)
    units = []
    for rb, (ks, _) in enumerate(geo):
        if units and geo[units[-1][0]][0] == ks:
            units[-1].append(rb)
        else:
            units.append([rb])
    assert max(len(unit) for unit in units) <= NA_MAX_MERGE

    def kslice(unit):
        ks = geo[unit[0]][0]
        return slice(ks * GRID_W, ks * GRID_W + NA_TK)

    row0 = pl.multiple_of(jnp.minimum(pl.program_id(1), 0), 8)

    def scores(u):
        unit = units[u]
        pw = len(unit) * 2 * NA_TQ
        blocks, tabs = [], []
        for g in range(NA_GROUP):
            parts = []
            for rb in unit:
                q = q_ref[0, rb * NA_TQ:(rb + 1) * NA_TQ, g * LANES:(g + 1) * LANES]
                zero = jnp.zeros_like(q)
                parts += [jnp.where(lane < NA_HEAD_DIM, q, zero), jnp.where(lane < NA_HEAD_DIM, zero, q)]
                tabs.append(tab_ref[0, g, geo[rb][1]])
            qg = jnp.concatenate(parts, axis=0)
            zg = jnp.zeros_like(qg)
            blocks.append(jnp.concatenate([qg if gg == g else zg for gg in range(NA_GROUP)], axis=1))
        qbd = jnp.concatenate(blocks, axis=0)
        s = lax.dot_general(k_ref[0, kslice(unit), :], qbd, nt, preferred_element_type=F32)
        s = s + jnp.concatenate(tabs, axis=1)
        s_refs[u % 2][:, 0:NA_GROUP * pw] = s
        return jnp.max(s, axis=0, keepdims=True)

    m = scores(0)
    for u, unit in enumerate(units):
        pw = len(unit) * 2 * NA_TQ
        m_next = scores(u + 1) if u + 1 < len(units) else None
        e = jnp.exp2(s_refs[u % 2][pl.ds(row0, NA_TK), 0:NA_GROUP * pw] - m).astype(BF16)
        for g in range(NA_GROUP):
            pv = jnp.dot(vt_ref[g * VT_ROWS:(g + 1) * VT_ROWS, kslice(unit)], e[:, g * pw:(g + 1) * pw],
                         preferred_element_type=F32)
            on = pv[:LANES] * (1.0 / pv[LANES:LANES + 1])
            for n, rb in enumerate(unit):
                h0 = on[:, 2 * n * NA_TQ:(2 * n + 1) * NA_TQ]
                h1 = on[:, (2 * n + 1) * NA_TQ:(2 * n + 2) * NA_TQ]
                ot = jnp.where(vdim < NA_HEAD_DIM, h0, h1)
                o_ref[0, rb * NA_TQ:(rb + 1) * NA_TQ, g * LANES:(g + 1) * LANES] = ot.T.astype(BF16)
        m = m_next


def _neigh_attention(z3, vt, tables, layer):
    b, s, _ = z3.shape
    rows = s // GRID_W
    gw = NA_GROUP * LANES
    return pl.pallas_call(
        functools.partial(_na_kernel, rows=rows),
        grid=(NA_HEADS // 2 // NA_GROUP, b),
        in_specs=[
            pl.BlockSpec((1, NA_GROUP, 3, NA_TK, 2 * NA_TQ), lambda hp, i: (layer, hp, 0, 0, 0)),
            pl.BlockSpec((1, s, gw), lambda hp, i: (i, 0, QN_BLK // NA_GROUP + hp)),
            pl.BlockSpec((1, s, gw), lambda hp, i: (i, 0, KN_BLK // NA_GROUP + hp)),
            pl.BlockSpec((NA_GROUP * VT_ROWS, s), lambda hp, i: (VT_NA_GROUP0 // NA_GROUP + hp, i)),
        ],
        out_specs=pl.BlockSpec((1, s, gw), lambda hp, i: (i, 0, hp)),
        out_shape=jax.ShapeDtypeStruct((b, s, NA_HEADS * NA_HEAD_DIM), BF16),
        scratch_shapes=[pltpu.VMEM((NA_TK, NA_GROUP * NA_MAX_MERGE * 2 * NA_TQ), F32)] * 2,
        compiler_params=pltpu.CompilerParams(
            dimension_semantics=("arbitrary", "arbitrary"), vmem_limit_bytes=VMEM_LIMIT_NA),
        name="neigh_attn",
    )(tables, z3, z3, vt)


def _gated_out(x_ref, oda_ref, ona_ref, gate_ref, w_ref):
    gate = gate_ref[...].astype(F32)
    o = jnp.concatenate([oda_ref[...], ona_ref[...]], axis=-1).astype(F32) * (gate * jax.nn.sigmoid(gate))
    return x_ref[...] + jnp.dot(o, w_ref[...], preferred_element_type=F32)


def _outproj_final_kernel(x_ref, oda_ref, ona_ref, gate_ref, w_ref, fg_ref, xo_ref):
    xo_ref[...] = _rms(_gated_out(x_ref, oda_ref, ona_ref, gate_ref, w_ref), fg_ref[...])


def _layer_boundary_kernel(x_ref, oda_ref, ona_ref, gate_ref, wout_ref, g_ref, win_ref,
                           xo_ref, z_ref, vt_ref):
    xo = _gated_out(x_ref, oda_ref, ona_ref, gate_ref, wout_ref)
    xo_ref[...] = xo
    _project_in(xo, g_ref, win_ref, z_ref, vt_ref)


def _token_specs(layer, tm=TM_PROJ):
    half = D_MODEL // 2
    return [
        pl.BlockSpec((tm, D_MODEL), lambda i: (i, 0)),
        pl.BlockSpec((tm, half), lambda i: (i, 0)),
        pl.BlockSpec((tm, half), lambda i: (i, 0)),
        pl.BlockSpec((tm, D_MODEL), lambda i: (i, GATE_BLK)),
        _resident((D_MODEL, D_MODEL), layer),
    ]


def _outproj_final(x2, oda2, ona2, z2, w_out, layer, final_g):
    m = x2.shape[0]
    return pl.pallas_call(
        _outproj_final_kernel,
        grid=(m // TM_FINAL,),
        in_specs=_token_specs(layer, TM_FINAL) + [pl.BlockSpec((1, D_MODEL), lambda i: (0, 0))],
        out_specs=pl.BlockSpec((TM_FINAL, D_MODEL), lambda i: (i, 0)),
        out_shape=jax.ShapeDtypeStruct((m, D_MODEL), F32),
        compiler_params=pltpu.CompilerParams(
            dimension_semantics=("arbitrary",), vmem_limit_bytes=VMEM_LIMIT),
        name="outproj_final",
    )(x2, oda2, ona2, z2, w_out, final_g)


def _layer_boundary(x2, oda2, ona2, z2, w_out, norm_g, w_in, layer):
    m = x2.shape[0]
    return pl.pallas_call(
        _layer_boundary_kernel,
        grid=(m // TM_PROJ,),
        in_specs=_token_specs(layer) + [_resident((1, D_MODEL), layer + 1),
                                        _resident((D_MODEL, IN_WIDTH), layer + 1)],
        out_specs=[pl.BlockSpec((TM_PROJ, D_MODEL), lambda i: (i, 0))] + _z_out_specs(),
        out_shape=[jax.ShapeDtypeStruct((m, D_MODEL), F32)] + _z_out_shapes(m),
        compiler_params=pltpu.CompilerParams(
            dimension_semantics=("arbitrary",), vmem_limit_bytes=VMEM_LIMIT_BOUNDARY),
        name="layer_boundary",
    )(x2, oda2, ona2, z2, w_out, norm_g, w_in)


def kernel(x, norm_g, w_in, w_out, lam_q1, lam_k1, lam_q2, lam_k2, subln_g, rpb, final_g):
    b, s, d = x.shape
    depth = norm_g.shape[0]
    assert d == D_MODEL and s % TQ_DA == 0 and s % (GRID_W * NA_QROWS) == 0
    rows = s // GRID_W
    assert rows >= NA_KROWS
    m = b * s
    slopes = np.asarray(2.0 ** (-8.0 * np.arange(1, DA_HEADS + 1) / DA_HEADS), dtype=np.float32)
    fg = final_g.reshape(1, d)
    norm_g3 = norm_g.reshape(depth, 1, d)
    x2 = x.reshape(m, d)
    z2, vt, tables = _inproj(x2, norm_g3, w_in, rpb, rows)
    for l in range(depth):
        lam_init = 0.8 - 0.6 * math.exp(-0.3 * l)
        scalars = jnp.asarray(np.concatenate(
            [slopes, np.asarray([lam_init, 1.0 - lam_init, 0.0, 0.0], np.float32)]))
        lamv = jnp.stack([lam_q1[l], lam_k1[l], lam_q2[l], lam_k2[l]])
        z3 = z2.reshape(b, s, IN_WIDTH)
        oda = _diff_attention(z3, vt, scalars, lamv, subln_g[l].reshape(-1, 1)).reshape(m, -1)
        ona = _neigh_attention(z3, vt, tables, l).reshape(m, -1)
        if l + 1 < depth:
            x2, z2, vt = _layer_boundary(x2, oda, ona, z2, w_out, norm_g3, w_in, l)
        else:
            x2 = _outproj_final(x2, oda, ona, z2, w_out, l, fg)
    return x2.reshape(b, s, d)
```

```python
import functools
import math

import numpy as np
import jax
import jax.numpy as jnp
from jax import lax
from jax.experimental import pallas as pl
from jax.experimental.pallas import tpu as pltpu

F32 = jnp.float32
BF16 = jnp.bfloat16

D_MODEL = 1024
GRID_W = 64
DA_HEADS = 4
DA_HEAD_DIM = 64
NA_HEADS = 8
NA_HEAD_DIM = 64
NA_ROWS_MAX = 8
NA_COLS = 16
RMS_EPS = 1e-6
IN_WIDTH = 4096
LANES = 128
SUBLANES = 8

QD_BLK, KD_BLK, VD_BLK = 0, 4, 8
QN_BLK, KN_BLK, VN_BLK = 12, 16, 20
GATE_BLK = 3

TM_PROJ = 512
TM_FINAL = 1024
FINAL_BUFFERS = 3
TQ_DA = 256
TK_DA = TQ_DA
SUM_ROWS = 16
NA_QROWS = 4
NA_KROWS = 11
NA_MAX_MERGE = 2
NA_GROUP = 2
assert QN_BLK % NA_GROUP == 0 and KN_BLK % NA_GROUP == 0
NA_TQ = NA_QROWS * GRID_W
NA_TK = NA_KROWS * GRID_W
N_RPB_DR = 2 * NA_ROWS_MAX - 1
N_RPB_DC = 2 * NA_COLS - 1
VMEM_LIMIT = 48 * 1024 * 1024
VMEM_LIMIT_BOUNDARY = 56 * 1024 * 1024
VMEM_LIMIT_NA = 56 * 1024 * 1024
LOG2E = math.log2(math.e)
Q_SCALE = DA_HEAD_DIM ** -0.5 * LOG2E
assert NA_HEAD_DIM == DA_HEAD_DIM
Z_CHUNK = 512
Q_CHUNKS = (QD_BLK * LANES // Z_CHUNK, QN_BLK * LANES // Z_CHUNK)
V_CHUNKS = (VD_BLK * LANES // Z_CHUNK, VN_BLK * LANES // Z_CHUNK)
VT_ROWS = LANES + SUM_ROWS
VT_GROUPS = len(V_CHUNKS) * Z_CHUNK // LANES
VT_NA_GROUP0 = Z_CHUNK // LANES


def _rms(x, g):
    return x * lax.rsqrt(jnp.mean(x * x, axis=-1, keepdims=True) + RMS_EPS) * g


def _project_in(x, g_ref, w_ref, z_ref, vt_ref):
    h = _rms(x, g_ref[...])
    ones = jnp.ones((SUM_ROWS, x.shape[0]), BF16)
    for j in range(IN_WIDTH // Z_CHUNK):
        cols = slice(j * Z_CHUNK, (j + 1) * Z_CHUNK)
        zc = jnp.dot(h, w_ref[:, cols], preferred_element_type=F32)
        if j in Q_CHUNKS:
            zc = zc * Q_SCALE
        z_ref[:, cols] = zc.astype(BF16)
        if j in V_CHUNKS:
            zt = zc.T.astype(BF16)
            for t in range(Z_CHUNK // LANES):
                grp = V_CHUNKS.index(j) * (Z_CHUNK // LANES) + t
                vt_ref[grp * VT_ROWS:grp * VT_ROWS + LANES, :] = zt[t * LANES:(t + 1) * LANES]
                vt_ref[grp * VT_ROWS + LANES:(grp + 1) * VT_ROWS, :] = ones


def _inproj_kernel(x_ref, g_ref, w_ref, rpb_ref, z_ref, vt_ref, tab_ref, dict_ref, *, rows):
    _project_in(x_ref[...], g_ref, w_ref, z_ref, vt_ref)
    _build_na_table(rpb_ref, tab_ref, dict_ref, rows)


def _resident(shape, layer):
    return pl.BlockSpec((None,) + shape, lambda i: (layer,) + (0,) * len(shape),
                        pipeline_mode=pl.Buffered(1))


def _inproj(x2, norm_g, w_in, rpb, rows):
    m = x2.shape[0]
    depth = rpb.shape[0]
    steps = m // TM_PROJ
    assert steps == depth * NA_HEADS, "one (layer, head) bias table is built per projection step"
    rpb_rev = jnp.pad(rpb[..., ::-1], ((0, 0), (0, 0), (0, 0), (0, LANES - N_RPB_DC)))
    rpb_rev = rpb_rev.reshape(depth * NA_HEADS, N_RPB_DR, LANES)
    tab_spec = pl.BlockSpec((1, 1, 3, NA_TK, NA_TQ),
                            lambda i: (i // NA_HEADS, (i % NA_HEADS) // 2, 0, 0, i % 2))
    tab_shape = jax.ShapeDtypeStruct((depth, NA_HEADS // 2, 3, NA_TK, 2 * NA_TQ), F32)
    return pl.pallas_call(
        functools.partial(_inproj_kernel, rows=rows),
        grid=(steps,),
        in_specs=[
            pl.BlockSpec((TM_PROJ, D_MODEL), lambda i: (i, 0)),
            _resident((1, D_MODEL), 0),
            _resident((D_MODEL, IN_WIDTH), 0),
            pl.BlockSpec((1, N_RPB_DR, LANES), lambda i: (i, 0, 0)),
        ],
        out_specs=_z_out_specs() + [tab_spec],
        out_shape=_z_out_shapes(m) + [tab_shape],
        scratch_shapes=[pltpu.VMEM((N_RPB_DR, GRID_W, GRID_W), F32)],
        compiler_params=pltpu.CompilerParams(
            dimension_semantics=("arbitrary",), vmem_limit_bytes=VMEM_LIMIT),
        name="inproj",
    )(x2, norm_g, w_in, rpb_rev)


def _z_out_specs():
    return [pl.BlockSpec((TM_PROJ, IN_WIDTH), lambda i: (i, 0)),
            pl.BlockSpec((VT_GROUPS * VT_ROWS, TM_PROJ), lambda i: (0, i))]


def _z_out_shapes(m):
    return [jax.ShapeDtypeStruct((m, IN_WIDTH), BF16),
            jax.ShapeDtypeStruct((VT_GROUPS * VT_ROWS, m), BF16)]


def _da_kernel(sc_ref, lamv_ref, g_ref, q_ref, k_ref, vt_ref, o_ref,
               bias_ref, sa_ref, sb_ref, pa_ref, pb_ref, *, seq):
    nq = seq // TQ_DA
    nk = seq // TK_DA
    h = pl.program_id(0)
    lam_init = sc_ref[4]
    out_scale = sc_ref[5]

    @pl.when(pl.program_id(1) == 0)
    def _():
        slope = sc_ref[h] * LOG2E
        kr = lax.broadcasted_iota(jnp.int32, (TK_DA, TQ_DA), 0)
        qc = lax.broadcasted_iota(jnp.int32, (TK_DA, TQ_DA), 1)
        for t in range(nq + nk - 1):
            bias_ref[t] = slope * jnp.abs(kr - qc + (t - (nq - 1)) * TQ_DA).astype(F32)

    lv = lamv_ref[...]
    lam = (jnp.exp(jnp.sum(lv[0:1] * lv[1:2], axis=-1, keepdims=True))
           - jnp.exp(jnp.sum(lv[2:3] * lv[3:4], axis=-1, keepdims=True)) + lam_init)
    lane = lax.broadcasted_iota(jnp.int32, (1, LANES), 1)
    g = g_ref[...]
    row0 = pl.multiple_of(jnp.minimum(pl.program_id(1), 0), 8)
    nt = (((1,), (1,)), ((), ()))

    def rows_of(blk):
        return slice(blk * TQ_DA, (blk + 1) * TQ_DA)

    def scores(blk, s_ref):
        q = q_ref[0, rows_of(blk), :]
        zero = jnp.zeros_like(q)
        qst = jnp.concatenate([jnp.where(lane < DA_HEAD_DIM, q, zero),
                               jnp.where(lane < DA_HEAD_DIM, zero, q)], axis=0)
        bias = jnp.concatenate([bias_ref[j - blk + nq - 1] for j in range(nk)], axis=0)
        s = lax.dot_general(k_ref[0], qst, nt, preferred_element_type=F32)
        s = s - jnp.concatenate([bias, bias], axis=1)
        s_ref[...] = s
        return jnp.max(s, axis=0, keepdims=True)

    def attend(s_ref, m, pv_ref):
        e = jnp.exp2(s_ref[pl.ds(row0, seq), :] - m)
        pv_ref[...] = jnp.dot(vt_ref[...], e.astype(BF16), preferred_element_type=F32)

    def finalize(blk, pv_ref):
        pv, l = pv_ref[0:LANES, :], pv_ref[LANES:LANES + 1, :]
        ot = pv[:, :TQ_DA] * (1.0 / l[:, :TQ_DA]) - pv[:, TQ_DA:] * (lam / l[:, TQ_DA:])
        ot = ot * lax.rsqrt(jnp.mean(ot * ot, axis=0, keepdims=True) + RMS_EPS) * g * out_scale
        o_ref[0, rows_of(blk), :] = ot.T.astype(BF16)

    def pair(p, m_even, first=False, last=False):
        if not first:
            finalize(2 * p - 1, pb_ref)
        m_odd = scores(2 * p + 1, sb_ref)
        attend(sa_ref, m_even, pa_ref)
        m_next = None if last else scores(2 * p + 2, sa_ref)
        finalize(2 * p, pa_ref)
        attend(sb_ref, m_odd, pb_ref)
        return m_next

    m_even = pair(0, scores(0, sa_ref), first=True)
    for p in range(1, nq // 2 - 1):
        m_even = pair(p, m_even)
    pair(nq // 2 - 1, m_even, last=True)
    finalize(nq - 1, pb_ref)


def _diff_attention(z3, vt, scalars, lamv, subln_g):
    b, s, _ = z3.shape
    smem = pl.BlockSpec(memory_space=pltpu.SMEM)
    nq, nk = s // TQ_DA, s // TK_DA
    return pl.pallas_call(
        functools.partial(_da_kernel, seq=s),
        grid=(DA_HEADS, b),
        in_specs=[
            smem,
            pl.BlockSpec((4, DA_HEAD_DIM), lambda h, i: (0, 0)),
            pl.BlockSpec((2 * DA_HEAD_DIM, 1), lambda h, i: (0, 0)),
            pl.BlockSpec((1, s, LANES), lambda h, i: (i, 0, QD_BLK + h)),
            pl.BlockSpec((1, s, LANES), lambda h, i: (i, 0, KD_BLK + h)),
            pl.BlockSpec((VT_ROWS, s), lambda h, i: (h, i)),
        ],
        out_specs=pl.BlockSpec((1, s, LANES), lambda h, i: (i, 0, h)),
        out_shape=jax.ShapeDtypeStruct((b, s, DA_HEADS * LANES), BF16),
        scratch_shapes=[
            pltpu.VMEM((nq + nk - 1, TK_DA, TQ_DA), F32),
            pltpu.VMEM((s, 2 * TQ_DA), F32),
            pltpu.VMEM((s, 2 * TQ_DA), F32),
            pltpu.VMEM((LANES + SUM_ROWS, 2 * TQ_DA), F32),
            pltpu.VMEM((LANES + SUM_ROWS, 2 * TQ_DA), F32),
        ],
        compiler_params=pltpu.CompilerParams(
            dimension_semantics=("arbitrary", "arbitrary"), vmem_limit_bytes=VMEM_LIMIT),
        name="diff_attn",
    )(scalars, lamv, subln_g, z3, z3, vt)


def _na_block_geometry(rows):
    kr = min(NA_ROWS_MAX, rows)
    nblk = rows // NA_QROWS
    geo = []
    for rb in range(nblk):
        ks = min(max(NA_QROWS * rb - kr // 2, 0), rows - NA_KROWS)
        var = 0 if rb == 0 else (2 if rb == nblk - 1 else 1)
        geo.append((ks, var))
    return geo


def _na_table_dr(rows):
    kr = min(NA_ROWS_MAX, rows)
    geo = _na_block_geometry(rows)
    rep = {0: 0, 1: 1, 2: len(geo) - 1}
    out = {}
    for var, rb in rep.items():
        ks = geo[rb][0]
        for ri in range(NA_QROWS):
            qr = NA_QROWS * rb + ri
            rs = min(max(qr - kr // 2, 0), rows - kr)
            for kj in range(NA_KROWS):
                krow = ks + kj
                out[(var, ri, kj)] = (krow - qr + NA_ROWS_MAX - 1) if rs <= krow < rs + kr else None
    return out


def _build_na_table(w_ref, tab_ref, dict_ref, rows):
    kc = lax.broadcasted_iota(jnp.int32, (GRID_W, GRID_W), 0)
    qc = lax.broadcasted_iota(jnp.int32, (GRID_W, GRID_W), 1)
    qcs = jnp.clip(qc - NA_COLS // 2, 0, GRID_W - NA_COLS)
    col_valid = (kc >= qcs) & (kc < qcs + NA_COLS)
    neg = jnp.full((GRID_W, GRID_W), -jnp.inf, F32)
    sub = lax.broadcasted_iota(jnp.int32, (SUBLANES, LANES), 0)
    for dr in range(N_RPB_DR):
        rowb = jnp.broadcast_to(w_ref[0, dr:dr + 1, :], (SUBLANES, LANES))
        w8 = rowb
        for r in range(1, SUBLANES):
            w8 = jnp.where(sub == r, pltpu.roll(rowb, r, 1), w8)
        t = jnp.concatenate([pltpu.roll(w8, (kc0 - (NA_COLS - 1)) % LANES, 1)
                             for kc0 in range(0, GRID_W, SUBLANES)], axis=0)[:, :GRID_W]
        dict_ref[dr] = jnp.where(col_valid, t * LOG2E, neg)
    for (var, ri, kj), dr in _na_table_dr(rows).items():
        tile = neg if dr is None else dict_ref[dr]
        tab_ref[0, 0, var, kj * GRID_W:(kj + 1) * GRID_W, ri * GRID_W:(ri + 1) * GRID_W] = tile


def _na_kernel(tab_ref, q_ref, k_ref, vt_ref, o_ref, sa_ref, sb_ref, *, rows):
    s_refs = (sa_ref, sb_ref)
    lane = lax.broadcasted_iota(jnp.int32, (1, LANES), 1)
    vdim = lax.broadcasted_iota(jnp.int32, (LANES, 1), 0)
    nt = (((1,), (1,)), ((), ()))
    geo = _na_block_geometry(rows)
    units = []
    for rb, (ks, _) in enumerate(geo):
        if units and geo[units[-1][0]][0] == ks:
            units[-1].append(rb)
        else:
            units.append([rb])
    assert max(len(unit) for unit in units) <= NA_MAX_MERGE

    def kslice(unit):
        ks = geo[unit[0]][0]
        return slice(ks * GRID_W, ks * GRID_W + NA_TK)

    row0 = pl.multiple_of(jnp.minimum(pl.program_id(1), 0), 8)

    def scores(u):
        unit = units[u]
        pw = len(unit) * 2 * NA_TQ
        blocks, tabs = [], []
        for g in range(NA_GROUP):
            parts = []
            for rb in unit:
                q = q_ref[0, rb * NA_TQ:(rb + 1) * NA_TQ, g * LANES:(g + 1) * LANES]
                zero = jnp.zeros_like(q)
                parts += [jnp.where(lane < NA_HEAD_DIM, q, zero), jnp.where(lane < NA_HEAD_DIM, zero, q)]
                tabs.append(tab_ref[0, g, geo[rb][1]])
            qg = jnp.concatenate(parts, axis=0)
            zg = jnp.zeros_like(qg)
            blocks.append(jnp.concatenate([qg if gg == g else zg for gg in range(NA_GROUP)], axis=1))
        qbd = jnp.concatenate(blocks, axis=0)
        s = lax.dot_general(k_ref[0, kslice(unit), :], qbd, nt, preferred_element_type=F32)
        s = s + jnp.concatenate(tabs, axis=1)
        s_refs[u % 2][:, 0:NA_GROUP * pw] = s
        return jnp.max(s, axis=0, keepdims=True)

    m = scores(0)
    for u, unit in enumerate(units):
        pw = len(unit) * 2 * NA_TQ
        m_next = scores(u + 1) if u + 1 < len(units) else None
        e = jnp.exp2(s_refs[u % 2][pl.ds(row0, NA_TK), 0:NA_GROUP * pw] - m).astype(BF16)
        for g in range(NA_GROUP):
            pv = jnp.dot(vt_ref[g * VT_ROWS:(g + 1) * VT_ROWS, kslice(unit)], e[:, g * pw:(g + 1) * pw],
                         preferred_element_type=F32)
            on = pv[:LANES] * (1.0 / pv[LANES:LANES + 1])
            for n, rb in enumerate(unit):
                h0 = on[:, 2 * n * NA_TQ:(2 * n + 1) * NA_TQ]
                h1 = on[:, (2 * n + 1) * NA_TQ:(2 * n + 2) * NA_TQ]
                ot = jnp.where(vdim < NA_HEAD_DIM, h0, h1)
                o_ref[0, rb * NA_TQ:(rb + 1) * NA_TQ, g * LANES:(g + 1) * LANES] = ot.T.astype(BF16)
        m = m_next


def _neigh_attention(z3, vt, tables, layer):
    b, s, _ = z3.shape
    rows = s // GRID_W
    gw = NA_GROUP * LANES
    return pl.pallas_call(
        functools.partial(_na_kernel, rows=rows),
        grid=(NA_HEADS // 2 // NA_GROUP, b),
        in_specs=[
            pl.BlockSpec((1, NA_GROUP, 3, NA_TK, 2 * NA_TQ), lambda hp, i: (layer, hp, 0, 0, 0)),
            pl.BlockSpec((1, s, gw), lambda hp, i: (i, 0, QN_BLK // NA_GROUP + hp)),
            pl.BlockSpec((1, s, gw), lambda hp, i: (i, 0, KN_BLK // NA_GROUP + hp)),
            pl.BlockSpec((NA_GROUP * VT_ROWS, s), lambda hp, i: (VT_NA_GROUP0 // NA_GROUP + hp, i)),
        ],
        out_specs=pl.BlockSpec((1, s, gw), lambda hp, i: (i, 0, hp)),
        out_shape=jax.ShapeDtypeStruct((b, s, NA_HEADS * NA_HEAD_DIM), BF16),
        scratch_shapes=[pltpu.VMEM((NA_TK, NA_GROUP * NA_MAX_MERGE * 2 * NA_TQ), F32)] * 2,
        compiler_params=pltpu.CompilerParams(
            dimension_semantics=("arbitrary", "arbitrary"), vmem_limit_bytes=VMEM_LIMIT_NA),
        name="neigh_attn",
    )(tables, z3, z3, vt)


def _gated_out(x_ref, oda_ref, ona_ref, gate_ref, w_ref):
    gate = gate_ref[...].astype(F32)
    o = jnp.concatenate([oda_ref[...], ona_ref[...]], axis=-1).astype(F32) * (gate * jax.nn.sigmoid(gate))
    return x_ref[...] + jnp.dot(o, w_ref[...], preferred_element_type=F32)


def _outproj_final_kernel(x_ref, oda_ref, ona_ref, gate_ref, w_ref, fg_ref, xo_ref):
    xo_ref[...] = _rms(_gated_out(x_ref, oda_ref, ona_ref, gate_ref, w_ref), fg_ref[...])


def _layer_boundary_kernel(x_ref, oda_ref, ona_ref, gate_ref, wout_ref, g_ref, win_ref,
                           xo_ref, z_ref, vt_ref):
    xo = _gated_out(x_ref, oda_ref, ona_ref, gate_ref, wout_ref)
    xo_ref[...] = xo
    _project_in(xo, g_ref, win_ref, z_ref, vt_ref)


def _token_specs(layer, tm=TM_PROJ):
    half = D_MODEL // 2
    return [
        pl.BlockSpec((tm, D_MODEL), lambda i: (i, 0)),
        pl.BlockSpec((tm, half), lambda i: (i, 0)),
        pl.BlockSpec((tm, half), lambda i: (i, 0)),
        pl.BlockSpec((tm, D_MODEL), lambda i: (i, GATE_BLK)),
        _resident((D_MODEL, D_MODEL), layer),
    ]


def _outproj_final_pipeline(x_ref, oda_ref, ona_ref, z_ref, w_ref, fg_ref, xo_ref, *, layer):
    m = x_ref.shape[0]
    half = D_MODEL // 2
    deep = pl.Buffered(FINAL_BUFFERS)
    pltpu.emit_pipeline(
        _outproj_final_kernel,
        grid=(m // TM_FINAL,),
        in_specs=[
            pl.BlockSpec((TM_FINAL, D_MODEL), lambda i: (i, 0), pipeline_mode=deep),
            pl.BlockSpec((TM_FINAL, half), lambda i: (i, 0), pipeline_mode=deep),
            pl.BlockSpec((TM_FINAL, half), lambda i: (i, 0), pipeline_mode=deep),
            pl.BlockSpec((TM_FINAL, D_MODEL), lambda i: (i, GATE_BLK), pipeline_mode=deep),
            pl.BlockSpec((D_MODEL, D_MODEL), lambda i: (0, 0)),
            pl.BlockSpec((1, D_MODEL), lambda i: (0, 0)),
        ],
        out_specs=[pl.BlockSpec((TM_FINAL, D_MODEL), lambda i: (i, 0))],
    )(x_ref, oda_ref, ona_ref, z_ref, w_ref.at[layer], fg_ref, xo_ref)


def _outproj_final(x2, oda2, ona2, z2, w_out, layer, final_g):
    m = x2.shape[0]
    anywhere = pl.BlockSpec(memory_space=pl.ANY)
    return pl.pallas_call(
        functools.partial(_outproj_final_pipeline, layer=layer),
        in_specs=[anywhere] * 6,
        out_specs=anywhere,
        out_shape=jax.ShapeDtypeStruct((m, D_MODEL), F32),
        compiler_params=pltpu.CompilerParams(vmem_limit_bytes=VMEM_LIMIT_BOUNDARY),
        name="outproj_final",
    )(x2, oda2, ona2, z2, w_out, final_g)


def _layer_boundary(x2, oda2, ona2, z2, w_out, norm_g, w_in, layer):
    m = x2.shape[0]
    return pl.pallas_call(
        _layer_boundary_kernel,
        grid=(m // TM_PROJ,),
        in_specs=_token_specs(layer) + [_resident((1, D_MODEL), layer + 1),
                                        _resident((D_MODEL, IN_WIDTH), layer + 1)],
        out_specs=[pl.BlockSpec((TM_PROJ, D_MODEL), lambda i: (i, 0))] + _z_out_specs(),
        out_shape=[jax.ShapeDtypeStruct((m, D_MODEL), F32)] + _z_out_shapes(m),
        compiler_params=pltpu.CompilerParams(
            dimension_semantics=("arbitrary",), vmem_limit_bytes=VMEM_LIMIT_BOUNDARY),
        name="layer_boundary",
    )(x2, oda2, ona2, z2, w_out, norm_g, w_in)


def kernel(x, norm_g, w_in, w_out, lam_q1, lam_k1, lam_q2, lam_k2, subln_g, rpb, final_g):
    b, s, d = x.shape
    depth = norm_g.shape[0]
    assert d == D_MODEL and s % TQ_DA == 0 and s % (GRID_W * NA_QROWS) == 0
    rows = s // GRID_W
    assert rows >= NA_KROWS
    m = b * s
    slopes = np.asarray(2.0 ** (-8.0 * np.arange(1, DA_HEADS + 1) / DA_HEADS), dtype=np.float32)
    fg = final_g.reshape(1, d)
    norm_g3 = norm_g.reshape(depth, 1, d)
    x2 = x.reshape(m, d)
    z2, vt, tables = _inproj(x2, norm_g3, w_in, rpb, rows)
    for l in range(depth):
        lam_init = 0.8 - 0.6 * math.exp(-0.3 * l)
        scalars = jnp.asarray(np.concatenate(
            [slopes, np.asarray([lam_init, 1.0 - lam_init, 0.0, 0.0], np.float32)]))
        lamv = jnp.stack([lam_q1[l], lam_k1[l], lam_q2[l], lam_k2[l]])
        z3 = z2.reshape(b, s, IN_WIDTH)
        oda = _diff_attention(z3, vt, scalars, lamv, subln_g[l].reshape(-1, 1)).reshape(m, -1)
        ona = _neigh_attention(z3, vt, tables, l).reshape(m, -1)
        if l + 1 < depth:
            x2, z2, vt = _layer_boundary(x2, oda, ona, z2, w_out, norm_g3, w_in, l)
        else:
            x2 = _outproj_final(x2, oda, ona, z2, w_out, l, fg)
    return x2.reshape(b, s, d)
```
